```python
import jax, jax.numpy as jnp
from jax import lax
import numpy as np

D_MODEL = 1024
BATCH = 16
SEQ = 2048
DEPTH = 1

PLE_DIM = 256
RET_HEADS = 4
RET_QK_DIM = 128
RET_V_DIM = 256
RET_QK_WIDTH = RET_HEADS * RET_QK_DIM
RET_V_WIDTH = RET_HEADS * RET_V_DIM
RET_CHUNK = 128
ROPE_BASE = 10000.0
CONV_CH = D_MODEL
CONV_WIDTH = 31
N_EXPERTS = 32
TOP_K = 4
D_FF = D_MODEL
SWIGLU_LIMIT = 7.0
SWIGLU_ALPHA = 1.702
MOE_BLOCK = 128
EPS = 1e-6
IN_WIDTHS = (RET_QK_WIDTH, RET_QK_WIDTH, RET_V_WIDTH, RET_V_WIDTH, CONV_CH, CONV_CH, D_MODEL, D_MODEL)
IN_WIDTH = 2 * RET_QK_WIDTH + 2 * RET_V_WIDTH + 2 * CONV_CH + 2 * D_MODEL

kernel_name = "hybrid_retention_conformer_moe_block"


def rms_norm(x, g):
    xf = x.astype(jnp.float32)
    y = xf * lax.rsqrt(jnp.mean(xf * xf, axis=-1, keepdims=True) + EPS)
    return (y * g.astype(jnp.float32)).astype(x.dtype)


def layer_norm(x, g, b):
    xf = x.astype(jnp.float32)
    mu = jnp.mean(xf, axis=-1, keepdims=True)
    var = jnp.mean(jnp.square(xf - mu), axis=-1, keepdims=True)
    y = (xf - mu) * lax.rsqrt(var + EPS)
    return (y * g.astype(jnp.float32) + b.astype(jnp.float32)).astype(x.dtype)


def rotary(t, pos):
    half = t.shape[-1] // 2
    inv_freq = ROPE_BASE ** (-jnp.arange(half, dtype=jnp.float32) / half)
    ang = pos[:, None] * inv_freq[None, :]
    cos = jnp.cos(ang)[None, :, None, :].astype(t.dtype)
    sin = jnp.sin(ang)[None, :, None, :].astype(t.dtype)
    t1, t2 = t[..., :half], t[..., half:]
    return jnp.concatenate([t1 * cos - t2 * sin, t2 * cos + t1 * sin], axis=-1)


def retention_chunkwise(q, k, v):
    b, s, h, dk = q.shape
    dv = v.shape[-1]
    c = RET_CHUNK
    n = s // c
    log_gamma = jnp.log1p(-jnp.exp2(-5.0 - jnp.arange(h, dtype=jnp.float32)))
    idx = jnp.arange(c, dtype=jnp.float32)
    rel = idx[:, None] - idx[None, :]
    decay_in = jnp.where(rel >= 0, jnp.exp(log_gamma[:, None, None] * jnp.maximum(rel, 0.0)), 0.0)
    decay_q = jnp.exp(log_gamma[:, None] * (idx + 1.0))[None, :, :, None]
    decay_k = jnp.exp(log_gamma[:, None] * (c - 1.0 - idx))[None, :, :, None]
    decay_chunk = jnp.exp(log_gamma * c)[None, :, None, None]

    def to_chunks(t):
        return t.reshape(b, n, c, h, t.shape[-1]).transpose(1, 0, 3, 2, 4)

    qc, kc, vc = to_chunks(q), to_chunks(k), to_chunks(v)

    def step(state, inp):
        qi, ki, vi = inp
        scores = jnp.einsum('bhid,bhjd->bhij', qi, ki) * decay_in
        inner = jnp.einsum('bhij,bhje->bhie', scores, vi)
        cross = jnp.einsum('bhid,bhde->bhie', qi, state) * decay_q
        state = state * decay_chunk + jnp.einsum('bhjd,bhje->bhde', ki * decay_k, vi)
        return state, inner + cross

    state0 = jnp.zeros((b, h, dk, dv), jnp.float32)
    _, out = lax.scan(step, state0, (qc, kc, vc))
    return out.transpose(1, 0, 3, 2, 4).reshape(b, s, h, dv)


def head_group_norm(o, g):
    mu = jnp.mean(o, axis=-1, keepdims=True)
    var = jnp.mean(jnp.square(o - mu), axis=-1, keepdims=True)
    y = (o - mu) * lax.rsqrt(var + EPS)
    b, s = o.shape[0], o.shape[1]
    return y.reshape(b, s, RET_V_WIDTH) * g.astype(jnp.float32)


def causal_depthwise_conv(u, w, bias):
    out = lax.conv_general_dilated(
        u, w[:, None, :].astype(u.dtype), window_strides=(1,),
        padding=((CONV_WIDTH - 1, 0),),
        dimension_numbers=('NWC', 'WIO', 'NWC'),
        feature_group_count=u.shape[-1])
    return out + bias


def moe(h, w_router, b_router, w_gu, b_gu, w_down, b_down):
    b, s, d = h.shape
    t = b * s
    hf = h.reshape(t, d)
    logits = (hf @ w_router + b_router).astype(jnp.float32)
    top_logit, top_idx = lax.top_k(logits, TOP_K)
    top_w = jax.nn.softmax(top_logit, axis=-1)
    n_assign = t * TOP_K
    flat_e = top_idx.reshape(-1)
    flat_w = top_w.reshape(-1)
    flat_tok = jnp.repeat(jnp.arange(t, dtype=jnp.int32), TOP_K)
    order = jnp.argsort(flat_e)
    sorted_e = flat_e[order]
    counts = jnp.bincount(flat_e, length=N_EXPERTS)
    padded = ((counts + MOE_BLOCK - 1) // MOE_BLOCK) * MOE_BLOCK
    start = jnp.cumsum(counts) - counts
    pend = jnp.cumsum(padded)
    pstart = pend - padded
    dest = pstart[sorted_e] + (jnp.arange(n_assign, dtype=jnp.int32) - start[sorted_e])
    cap = n_assign + N_EXPERTS * MOE_BLOCK
    n_blocks = cap // MOE_BLOCK
    buf_tok = jnp.zeros((cap,), jnp.int32).at[dest].set(flat_tok[order])
    buf_w = jnp.zeros((cap,), jnp.float32).at[dest].set(flat_w[order])
    block_e = jnp.minimum(
        jnp.searchsorted(pend, jnp.arange(n_blocks, dtype=pend.dtype) * MOE_BLOCK, side='right'),
        N_EXPERTS - 1)
    xb = hf[buf_tok].reshape(n_blocks, MOE_BLOCK, d)

    def expert_block(args):
        xe, e = args
        gu = xe @ w_gu[e] + b_gu[e]
        gate, up = gu[:, :D_FF], gu[:, D_FF:]
        gate = jnp.minimum(gate, SWIGLU_LIMIT)
        up = jnp.clip(up, -SWIGLU_LIMIT, SWIGLU_LIMIT)
        act = (up + 1.0) * gate * jax.nn.sigmoid(SWIGLU_ALPHA * gate)
        return act @ w_down[e] + b_down[e]

    yb = lax.map(expert_block, (xb, block_e)).reshape(cap, d)
    y = jax.ops.segment_sum(yb * buf_w[:, None].astype(yb.dtype), buf_tok, num_segments=t)
    return y.reshape(b, s, d)


def setup_inputs(seed: int = 0) -> dict:
    key = jax.random.key(seed)
    ks = jax.random.split(key, 24)
    f32 = jnp.float32
    L, D = DEPTH, D_MODEL

    def nrm(k, shape, fan_in):
        return jax.random.normal(k, shape, f32) * (fan_in ** -0.5)

    def gain(k, shape):
        return 1.0 + 0.02 * jax.random.normal(k, shape, f32)

    def small(k, shape):
        return 0.01 * jax.random.normal(k, shape, f32)

    return {
        "x": jax.random.normal(ks[0], (BATCH, SEQ, D), f32),
        "p": jax.random.normal(ks[1], (DEPTH, BATCH, SEQ, PLE_DIM), f32),
        "g_mix": gain(ks[2], (L, D)),
        "w_in": nrm(ks[3], (L, D, IN_WIDTH), D),
        "ret_gn_w": gain(ks[4], (L, RET_V_WIDTH)),
        "w_ret_out": nrm(ks[5], (L, RET_V_WIDTH, D), RET_V_WIDTH),
        "w_dw": nrm(ks[6], (L, CONV_WIDTH, CONV_CH), CONV_WIDTH),
        "b_dw": small(ks[7], (L, CONV_CH)),
        "conv_ln_w": gain(ks[8], (L, CONV_CH)),
        "conv_ln_b": small(ks[9], (L, CONV_CH)),
        "w_conv_out": nrm(ks[10], (L, CONV_CH, D), CONV_CH),
        "b_conv_out": small(ks[11], (L, D)),
        "w_o": nrm(ks[12], (L, D, D), D),
        "g_ffn": gain(ks[13], (L, D)),
        "w_router": nrm(ks[14], (L, D, N_EXPERTS), D),
        "b_router": small(ks[15], (L, N_EXPERTS)),
        "w_gu": nrm(ks[16], (L, N_EXPERTS, D, 2 * D_FF), D),
        "b_gu": small(ks[17], (L, N_EXPERTS, 2 * D_FF)),
        "w_down": nrm(ks[18], (L, N_EXPERTS, D_FF, D), D_FF),
        "b_down": small(ks[19], (L, N_EXPERTS, D)),
        "g_ple": gain(ks[20], (L, D)),
        "w_ple_gate": nrm(ks[21], (L, D, D), D),
        "w_ple_proj": nrm(ks[22], (L, PLE_DIM, D), PLE_DIM),
        "g_final": gain(ks[23], (D,)),
    }


def reference(x, p, g_mix, w_in, ret_gn_w, w_ret_out, w_dw, b_dw, conv_ln_w, conv_ln_b,
              w_conv_out, b_conv_out, w_o, g_ffn, w_router, b_router, w_gu, b_gu,
              w_down, b_down, g_ple, w_ple_gate, w_ple_proj, g_final):
    b, s, _ = x.shape
    pos = jnp.arange(s, dtype=jnp.float32)
    splits = np.cumsum(IN_WIDTHS)[:-1].tolist()
    for i in range(DEPTH):
        h = rms_norm(x, g_mix[i])
        u = h @ w_in[i]
        q, k, v, g_ret, glu_a, glu_b, m_ret, m_conv = jnp.split(u, splits, axis=-1)

        q = rotary(q.reshape(b, s, RET_HEADS, RET_QK_DIM), pos)
        k = rotary(k.reshape(b, s, RET_HEADS, RET_QK_DIM), pos) * (RET_QK_DIM ** -0.5)
        v = v.reshape(b, s, RET_HEADS, RET_V_DIM)
        ret = retention_chunkwise(q.astype(jnp.float32), k.astype(jnp.float32), v.astype(jnp.float32))
        ret = head_group_norm(ret, ret_gn_w[i]).astype(x.dtype)
        y_ret = (jax.nn.silu(g_ret) * ret) @ w_ret_out[i]

        c = glu_a * jax.nn.sigmoid(glu_b)
        c = causal_depthwise_conv(c, w_dw[i], b_dw[i])
        c = jax.nn.silu(layer_norm(c, conv_ln_w[i], conv_ln_b[i]))
        y_conv = c @ w_conv_out[i] + b_conv_out[i]

        mixed = jax.nn.sigmoid(m_ret) * y_ret + jax.nn.sigmoid(m_conv) * y_conv
        x = x + mixed @ w_o[i]

        x = x + moe(rms_norm(x, g_ffn[i]), w_router[i], b_router[i], w_gu[i], b_gu[i],
                    w_down[i], b_down[i])

        gate = jax.nn.sigmoid(rms_norm(x, g_ple[i]) @ w_ple_gate[i])
        x = x + gate * (p[i] @ w_ple_proj[i])
    return rms_norm(x, g_final)
```

```python
import functools

import numpy as np
import jax
import jax.numpy as jnp
from jax import lax
from jax.experimental import pallas as pl
from jax.experimental.pallas import tpu as pltpu

F32 = jnp.float32
BF16 = jnp.bfloat16
I32 = jnp.int32

EPS = 1e-6
RET_HEADS = 4
RET_QK_DIM = 128
RET_V_DIM = 256
RET_CHUNK = 128
ROPE_BASE = 10000.0
CONV_WIDTH = 31
N_EXPERTS = 32
TOP_K = 4
SWIGLU_LIMIT = 7.0
SWIGLU_ALPHA = 1.702

LANES = 128
SUBLANES = 8
MIB = 1024 * 1024

TM_PROJ = 512
TN_PROJ = 1024
TS_RET = 512
TS_CONV = 512
CONV_ROWS = 32
HALO = 32
TM_MIX = 512
TM_DISP = 512
MOE_BLOCK = 256
TM_COMB = 256


def _params(vmem_mib, n_axes=1):
    return pltpu.CompilerParams(
        dimension_semantics=("arbitrary",) * n_axes,
        vmem_limit_bytes=int(vmem_mib * MIB),
    )


def _const_spec(shape):
    nd = len(shape)
    return pl.BlockSpec(shape, lambda *_: (0,) * nd)


def _sigmoid(v):
    return 1.0 / (1.0 + jnp.exp(-v))


def _rms(v, g):
    ms = jnp.mean(v * v, axis=-1, keepdims=True)
    return v * lax.rsqrt(ms + EPS) * g


def _in_proj_kernel(x_ref, g_ref, w_ref, u_ref, *, tn):
    h = _rms(x_ref[...], g_ref[...]).astype(BF16)
    for j in range(u_ref.shape[1] // tn):
        sl = slice(j * tn, (j + 1) * tn)
        u_ref[:, sl] = jnp.dot(h, w_ref[:, sl], preferred_element_type=F32).astype(BF16)


def _in_proj(x2, g, w_bf):
    t, d = x2.shape
    n = w_bf.shape[1]
    tm = TM_PROJ
    return pl.pallas_call(
        functools.partial(_in_proj_kernel, tn=TN_PROJ),
        grid=(t // tm,),
        in_specs=[
            pl.BlockSpec((tm, d), lambda i: (i, 0)),
            _const_spec((1, d)),
            _const_spec((d, n)),
        ],
        out_specs=pl.BlockSpec((tm, n), lambda i: (i, 0)),
        out_shape=jax.ShapeDtypeStruct((t, n), BF16),
        compiler_params=_params(56),
        name="in_proj",
    )(x2, g, w_bf)


def _retention_kernel(q_ref, k_ref, v_ref, g_ref, cos_ref, sin_ref, din_ref, dq_ref, dk_ref,
                      dc_ref, gn_ref, o_ref, state_ref, *, chunk):
    @pl.when(pl.program_id(1) == 0)
    def _():
        state_ref[...] = jnp.zeros_like(state_ref)

    ts = q_ref.shape[1]
    scale = RET_QK_DIM ** -0.5

    def chunk_body(c, carry):
        r = pl.ds(pl.multiple_of(c * chunk, chunk), chunk)
        cos = cos_ref[r, :]
        sin = sin_ref[r, :]
        for h in range(RET_HEADS):
            qs = slice(h * RET_QK_DIM, (h + 1) * RET_QK_DIM)
            vs = slice(h * RET_V_DIM, (h + 1) * RET_V_DIM)
            q = q_ref[0, r, qs].astype(F32)
            k = k_ref[0, r, qs].astype(F32)
            qr = q * cos + pltpu.roll(q, RET_QK_DIM // 2, 1) * sin
            kr = (k * cos + pltpu.roll(k, RET_QK_DIM // 2, 1) * sin) * scale
            qb = qr.astype(BF16)
            v = v_ref[0, r, vs]
            scores = lax.dot_general(qb, kr.astype(BF16), (((1,), (1,)), ((), ())),
                                     preferred_element_type=F32) * din_ref[h]
            inner = jnp.dot(scores.astype(BF16), v, preferred_element_type=F32)
            st = state_ref[h]
            cross = jnp.dot(qb, st.astype(BF16), preferred_element_type=F32) * dq_ref[h]
            kd = (kr * dk_ref[h]).astype(BF16)
            state_ref[h] = st * dc_ref[h] + lax.dot_general(
                kd, v, (((0,), (0,)), ((), ())), preferred_element_type=F32)
            o = inner + cross
            mu = jnp.mean(o, axis=-1, keepdims=True)
            oc = o - mu
            var = jnp.mean(oc * oc, axis=-1, keepdims=True)
            y = oc * lax.rsqrt(var + EPS) * gn_ref[:, vs]
            gate = g_ref[0, r, vs].astype(F32)
            o_ref[0, r, vs] = (gate * _sigmoid(gate) * y).astype(BF16)
        return carry

    lax.fori_loop(0, ts // chunk, chunk_body, 0)


def _retention_tables(s):
    half = RET_QK_DIM // 2
    pos = jnp.arange(s, dtype=F32)
    inv_freq = ROPE_BASE ** (-jnp.arange(half, dtype=F32) / half)
    ang = pos[:, None] * inv_freq[None, :]
    cos = jnp.cos(ang)
    sin = jnp.sin(ang)
    cos2 = jnp.concatenate([cos, cos], axis=-1)
    sin2 = jnp.concatenate([-sin, sin], axis=-1)
    c = RET_CHUNK
    h = RET_HEADS
    log_gamma = jnp.log1p(-jnp.exp2(-5.0 - jnp.arange(h, dtype=F32)))
    idx = jnp.arange(c, dtype=F32)
    rel = idx[:, None] - idx[None, :]
    din = jnp.where(rel >= 0, jnp.exp(log_gamma[:, None, None] * jnp.maximum(rel, 0.0)), 0.0)
    dq = jnp.exp(log_gamma[:, None] * (idx + 1.0))
    dk = jnp.exp(log_gamma[:, None] * (c - 1.0 - idx))
    dc = jnp.exp(log_gamma * c)
    dq_b = jnp.broadcast_to(dq[:, :, None], (h, c, RET_V_DIM))
    dk_b = jnp.broadcast_to(dk[:, :, None], (h, c, RET_QK_DIM))
    dc_b = jnp.broadcast_to(dc[:, None, None], (h, RET_QK_DIM, RET_V_DIM))
    return cos2, sin2, din, dq_b, dk_b, dc_b


def _retention(u3, gn_w):
    b, s, _ = u3.shape
    ts = TS_RET
    qk_w = RET_HEADS * RET_QK_DIM
    v_w = RET_HEADS * RET_V_DIM
    cos2, sin2, din, dq_b, dk_b, dc_b = _retention_tables(s)
    q_blk, k_blk = 0, 1
    v_blk = (2 * qk_w) // v_w
    g_blk = v_blk + 1
    seq_spec = lambda w, blk: pl.BlockSpec((1, ts, w), lambda i, j: (i, j, blk))
    return pl.pallas_call(
        functools.partial(_retention_kernel, chunk=RET_CHUNK),
        grid=(b, s // ts),
        in_specs=[
            seq_spec(qk_w, q_blk), seq_spec(qk_w, k_blk), seq_spec(v_w, v_blk), seq_spec(v_w, g_blk),
            pl.BlockSpec((ts, RET_QK_DIM), lambda i, j: (j, 0)),
            pl.BlockSpec((ts, RET_QK_DIM), lambda i, j: (j, 0)),
            _const_spec(din.shape), _const_spec(dq_b.shape), _const_spec(dk_b.shape),
            _const_spec(dc_b.shape), _const_spec((1, v_w)),
        ],
        out_specs=pl.BlockSpec((1, ts, v_w), lambda i, j: (i, j, 0)),
        out_shape=jax.ShapeDtypeStruct((b, s, v_w), BF16),
        scratch_shapes=[pltpu.VMEM((RET_HEADS, RET_QK_DIM, RET_V_DIM), F32)],
        compiler_params=_params(32, 2),
        name="retention",
    )(u3, u3, u3, u3, cos2, sin2, din, dq_b, dk_b, dc_b, gn_w)


def _conv_kernel(a_ref, b_ref, w_ref, bias_ref, lnw_ref, lnb_ref, o_ref, ext_ref, conv_ref):
    ts = a_ref.shape[1]
    ch = a_ref.shape[2]
    first = HALO - (CONV_WIDTH - 1)

    @pl.when(pl.program_id(1) == 0)
    def _():
        ext_ref[0:HALO, :] = jnp.zeros((HALO, ch), F32)

    @pl.when(pl.program_id(1) > 0)
    def _():
        ext_ref[0:HALO, :] = ext_ref[ts:ts + HALO, :]

    ext_ref[HALO:HALO + ts, :] = a_ref[0].astype(F32) * _sigmoid(b_ref[0].astype(F32))

    for l in range(ch // LANES):
        ls = slice(l * LANES, (l + 1) * LANES)
        taps = [w_ref[j:j + 1, ls] for j in range(CONV_WIDTH)]
        bias = bias_ref[:, ls]

        def row_body(i, carry):
            r0 = pl.multiple_of(i * CONV_ROWS, CONV_ROWS)
            win = ext_ref[pl.ds(r0, CONV_ROWS + HALO), ls]
            acc = jnp.broadcast_to(bias, (CONV_ROWS, LANES))
            for r in range(SUBLANES):
                offs = [o for o in range(first, first + CONV_WIDTH) if o % SUBLANES == r]
                span = offs[-1] - r + CONV_ROWS
                sh = win[r:r + span]
                for o in offs:
                    acc = acc + taps[o - first] * sh[o - r:o - r + CONV_ROWS]
            conv_ref[pl.ds(r0, CONV_ROWS), ls] = acc
            return carry

        lax.fori_loop(0, ts // CONV_ROWS, row_body, 0)

    c = conv_ref[...]
    mu = jnp.mean(c, axis=-1, keepdims=True)
    cc = c - mu
    var = jnp.mean(cc * cc, axis=-1, keepdims=True)
    y = cc * lax.rsqrt(var + EPS) * lnw_ref[...] + lnb_ref[...]
    o_ref[0] = (y * _sigmoid(y)).astype(BF16)


def _conv_branch(u3, w_dw, b_dw, ln_w, ln_b, a_blk, b_blk):
    b, s, _ = u3.shape
    ch = w_dw.shape[1]
    ts = TS_CONV
    seq_spec = lambda blk: pl.BlockSpec((1, ts, ch), lambda i, j: (i, j, blk))
    return pl.pallas_call(
        _conv_kernel,
        grid=(b, s // ts),
        in_specs=[
            seq_spec(a_blk), seq_spec(b_blk),
            _const_spec((CONV_WIDTH, ch)), _const_spec((1, ch)), _const_spec((1, ch)), _const_spec((1, ch)),
        ],
        out_specs=pl.BlockSpec((1, ts, ch), lambda i, j: (i, j, 0)),
        out_shape=jax.ShapeDtypeStruct((b, s, ch), BF16),
        scratch_shapes=[pltpu.VMEM((ts + HALO, ch), F32), pltpu.VMEM((ts, ch), F32)],
        compiler_params=_params(32, 2),
        name="conv_branch",
    )(u3, u3, w_dw, b_dw, ln_w, ln_b)


def _mix_route_kernel(gret_ref, cact_ref, mret_ref, mconv_ref, x_ref, wro_ref, wco_ref, bco_ref,
                      wo_ref, gffn_ref, wrh_ref, wrl_ref, br_ref, tri_ref,
                      x1_ref, h2_ref, ri_ref, rw_ref, cnt_ref, carry_ref):
    @pl.when(pl.program_id(0) == 0)
    def _():
        carry_ref[...] = jnp.zeros_like(carry_ref)

    tm = x_ref.shape[0]
    y_ret = jnp.dot(gret_ref[...], wro_ref[...], preferred_element_type=F32)
    y_conv = jnp.dot(cact_ref[...], wco_ref[...], preferred_element_type=F32) + bco_ref[...]
    mixed = (_sigmoid(mret_ref[...].astype(F32)) * y_ret
             + _sigmoid(mconv_ref[...].astype(F32)) * y_conv)
    x1 = x_ref[...] + jnp.dot(mixed.astype(BF16), wo_ref[...], preferred_element_type=F32)
    x1_ref[...] = x1
    h2 = _rms(x1, gffn_ref[...])
    h2_ref[...] = h2

    hh = h2.astype(BF16)
    hl = (h2 - hh.astype(F32)).astype(BF16)
    logits = (jnp.dot(hh, wrh_ref[...], preferred_element_type=F32)
              + jnp.dot(hh, wrl_ref[...], preferred_element_type=F32)
              + jnp.dot(hl, wrh_ref[...], preferred_element_type=F32)) + br_ref[...]
    lane = lax.broadcasted_iota(I32, (tm, LANES), 1)
    vals = jnp.where(lane < N_EXPERTS, logits, -jnp.inf)

    idxs, tops = [], []
    for _ in range(TOP_K):
        m = jnp.max(vals, axis=-1, keepdims=True)
        i = jnp.min(jnp.where(vals == m, lane, LANES), axis=-1, keepdims=True)
        idxs.append(i)
        tops.append(m)
        vals = jnp.where(lane == i, -jnp.inf, vals)
    exps = [jnp.exp(m - tops[0]) for m in tops]
    denom = exps[0] + exps[1] + exps[2] + exps[3]
    wts = [e / denom for e in exps]

    onehot = jnp.zeros((tm, LANES), F32)
    for i in idxs:
        onehot = onehot + (lane == i).astype(F32)
    before = jnp.dot(tri_ref[...], onehot.astype(BF16), preferred_element_type=F32) + carry_ref[...]
    ranks = [jnp.sum(jnp.where(lane == i, before, 0.0), axis=-1, keepdims=True).astype(I32) for i in idxs]
    carry = carry_ref[...] + jnp.sum(onehot, axis=0, keepdims=True)
    carry_ref[...] = carry
    cnt_ref[...] = jnp.broadcast_to(carry, cnt_ref.shape).astype(I32)

    ri = jnp.zeros((tm, LANES), I32)
    rw = jnp.zeros((tm, LANES), F32)
    for k in range(TOP_K):
        ri = jnp.where(lane == k, idxs[k], ri)
        ri = jnp.where(lane == TOP_K + k, ranks[k], ri)
        rw = jnp.where(lane == k, wts[k], rw)
    ri_ref[...] = ri
    rw_ref[...] = rw


def _mix_route(gret, cact, u, x2, wro, wco, bco, wo, gffn, wr_hi, wr_lo, br, mret_blk, mconv_blk):
    t, d = x2.shape
    tm = TM_MIX
    tri = jnp.asarray(np.tril(np.ones((tm, tm), np.float32), -1), BF16)
    row = lambda w: pl.BlockSpec((tm, w), lambda i: (i, 0))
    col = lambda blk: pl.BlockSpec((tm, d), lambda i: (i, blk))
    return pl.pallas_call(
        _mix_route_kernel,
        grid=(t // tm,),
        in_specs=[
            row(d), row(d), col(mret_blk), col(mconv_blk), row(d),
            _const_spec((d, d)), _const_spec((d, d)), _const_spec((1, d)), _const_spec((d, d)),
            _const_spec((1, d)), _const_spec((d, LANES)), _const_spec((d, LANES)), _const_spec((1, LANES)),
            _const_spec((tm, tm)),
        ],
        out_specs=[row(d), row(d), row(LANES), row(LANES), _const_spec((SUBLANES, LANES))],
        out_shape=[
            jax.ShapeDtypeStruct((t, d), F32), jax.ShapeDtypeStruct((t, d), F32),
            jax.ShapeDtypeStruct((t, LANES), I32), jax.ShapeDtypeStruct((t, LANES), F32),
            jax.ShapeDtypeStruct((SUBLANES, LANES), I32),
        ],
        scratch_shapes=[pltpu.VMEM((1, LANES), F32)],
        compiler_params=_params(48),
        name="mix_route",
    )(gret, cact, u, u, x2, wro, wco, bco, wo, gffn, wr_hi, wr_lo, br, tri)


def _dispatch_kernel(dest_ref, h2_ref, xb_in_ref, xb_ref, sem):
    del xb_in_ref
    tm = h2_ref.shape[0]

    def row_copy(t, k):
        return pltpu.make_async_copy(
            h2_ref.at[pl.ds(t, 1)], xb_ref.at[pl.ds(dest_ref[0, 0, t * TOP_K + k], 1)], sem)

    def issue(t, carry):
        for k in range(TOP_K):
            row_copy(t, k).start()
        return carry

    lax.fori_loop(0, tm, issue, 0)

    def drain(t, carry):
        for k in range(TOP_K):
            row_copy(t, k).wait()
        return carry

    lax.fori_loop(0, tm, drain, 0)


def _dispatch(dest3, h2, cap):
    t, d = h2.shape
    tm = TM_DISP
    xb0 = jnp.zeros((cap, d), F32)
    return pl.pallas_call(
        _dispatch_kernel,
        grid=(t // tm,),
        in_specs=[
            pl.BlockSpec((1, 1, tm * TOP_K), lambda i: (i, 0, 0), memory_space=pltpu.SMEM),
            pl.BlockSpec((tm, d), lambda i: (i, 0)),
            pl.BlockSpec(memory_space=pl.ANY),
        ],
        out_specs=pl.BlockSpec(memory_space=pl.ANY),
        out_shape=jax.ShapeDtypeStruct((cap, d), F32),
        scratch_shapes=[pltpu.SemaphoreType.DMA],
        input_output_aliases={2: 0},
        compiler_params=_params(16),
        name="dispatch",
    )(dest3, h2, xb0)


def _expert_kernel(be_ref, nu_ref, xb_ref, wgu_ref, bgu_ref, wd_ref, bd_ref, yb_ref):
    del be_ref
    b = pl.program_id(0)
    dff = wd_ref.shape[1]

    @pl.when(b < nu_ref[0])
    def _():
        x = xb_ref[...].astype(BF16)
        gu = jnp.dot(x, wgu_ref[0], preferred_element_type=F32) + bgu_ref[0]
        gate = jnp.minimum(gu[:, :dff], SWIGLU_LIMIT)
        up = jnp.clip(gu[:, dff:], -SWIGLU_LIMIT, SWIGLU_LIMIT)
        act = (up + 1.0) * gate * _sigmoid(SWIGLU_ALPHA * gate)
        yb_ref[...] = jnp.dot(act.astype(BF16), wd_ref[0], preferred_element_type=F32) + bd_ref[0]

    @pl.when(b >= nu_ref[0])
    def _():
        yb_ref[...] = jnp.zeros_like(yb_ref)


def _experts(block_e, n_used, xb, wgu, bgu, wd, bd):
    cap, d = xb.shape
    e, _, dff2 = wgu.shape
    blk = MOE_BLOCK
    grid_spec = pltpu.PrefetchScalarGridSpec(
        num_scalar_prefetch=2,
        grid=(cap // blk,),
        in_specs=[
            pl.BlockSpec((blk, d), lambda b, be, nu: (b, 0)),
            pl.BlockSpec((1, d, dff2), lambda b, be, nu: (be[b], 0, 0)),
            pl.BlockSpec((1, 1, dff2), lambda b, be, nu: (be[b], 0, 0)),
            pl.BlockSpec((1, dff2 // 2, d), lambda b, be, nu: (be[b], 0, 0)),
            pl.BlockSpec((1, 1, d), lambda b, be, nu: (be[b], 0, 0)),
        ],
        out_specs=pl.BlockSpec((blk, d), lambda b, be, nu: (b, 0)),
    )
    return pl.pallas_call(
        _expert_kernel,
        grid_spec=grid_spec,
        out_shape=jax.ShapeDtypeStruct((cap, d), F32),
        compiler_params=_params(48),
        name="experts",
    )(block_e, n_used, xb, wgu, bgu, wd, bd)


def _combine_kernel(dest_ref, x1_ref, rw_ref, p_ref, gple_ref, wpg_ref, wpp_ref, gfin_ref, yb_ref,
                    out_ref, buf_ref, sem):
    tm = x1_ref.shape[0]

    def row_copy(t, k):
        return pltpu.make_async_copy(
            yb_ref.at[pl.ds(dest_ref[0, 0, t * TOP_K + k], 1)], buf_ref.at[k, pl.ds(t, 1)], sem)

    def issue(t, carry):
        for k in range(TOP_K):
            row_copy(t, k).start()
        return carry

    lax.fori_loop(0, tm, issue, 0)

    def drain(t, carry):
        for k in range(TOP_K):
            row_copy(t, k).wait()
        return carry

    lax.fori_loop(0, tm, drain, 0)

    rw = rw_ref[...]
    y = rw[:, 0:1] * buf_ref[0]
    for k in range(1, TOP_K):
        y = y + rw[:, k:k + 1] * buf_ref[k]
    x2 = x1_ref[...] + y
    h3 = _rms(x2, gple_ref[...]).astype(BF16)
    gate = _sigmoid(jnp.dot(h3, wpg_ref[...], preferred_element_type=F32))
    proj = jnp.dot(p_ref[...].astype(BF16), wpp_ref[...], preferred_element_type=F32)
    x3 = x2 + gate * proj
    out_ref[...] = _rms(x3, gfin_ref[...])


def _combine(dest3, x1, rw, p2, gple, wpg, wpp, gfin, yb):
    t, d = x1.shape
    pd = p2.shape[1]
    tm = TM_COMB
    row = lambda w: pl.BlockSpec((tm, w), lambda i: (i, 0))
    return pl.pallas_call(
        _combine_kernel,
        grid=(t // tm,),
        in_specs=[
            pl.BlockSpec((1, 1, tm * TOP_K), lambda i: (i, 0, 0), memory_space=pltpu.SMEM),
            row(d), row(LANES), row(pd),
            _const_spec((1, d)), _const_spec((d, d)), _const_spec((pd, d)), _const_spec((1, d)),
            pl.BlockSpec(memory_space=pl.ANY),
        ],
        out_specs=row(d),
        out_shape=jax.ShapeDtypeStruct((t, d), F32),
        scratch_shapes=[pltpu.VMEM((TOP_K, tm, d), F32), pltpu.SemaphoreType.DMA],
        compiler_params=_params(32),
        name="combine",
    )(dest3, x1, rw, p2, gple, wpg, wpp, gfin, yb)


def _route_plan(ri, cnt, n_assign):
    blk = MOE_BLOCK
    idx = ri[:, :TOP_K]
    rank = ri[:, TOP_K:2 * TOP_K]
    counts = cnt[0, :N_EXPERTS]
    padded = ((counts + blk - 1) // blk) * blk
    pend = jnp.cumsum(padded)
    pstart = pend - padded
    dest = pstart[idx] + rank
    cap = n_assign + N_EXPERTS * blk
    n_blocks = cap // blk
    block_e = jnp.minimum(
        jnp.searchsorted(pend, jnp.arange(n_blocks, dtype=pend.dtype) * blk, side="right"),
        N_EXPERTS - 1).astype(I32)
    n_used = (pend[-1:] // blk).astype(I32)
    return dest.astype(I32), block_e, n_used, cap


def _layer(x, p_i, g_mix, w_in, ret_gn_w, w_ret_out, w_dw, b_dw, conv_ln_w, conv_ln_b, w_conv_out,
           b_conv_out, w_o, g_ffn, w_router, b_router, w_gu, b_gu, w_down, b_down, g_ple,
           w_ple_gate, w_ple_proj, g_final):
    b, s, d = x.shape
    t = b * s
    x2 = x.reshape(t, d)
    row = lambda v: v.reshape(1, -1)

    u = _in_proj(x2, row(g_mix), w_in.astype(BF16))
    u3 = u.reshape(b, s, -1)
    gret = _retention(u3, row(ret_gn_w)).reshape(t, -1)
    a_blk, b_blk, mret_blk, mconv_blk = 3, 4, 5, 6
    cact = _conv_branch(u3, w_dw, row(b_dw), row(conv_ln_w), row(conv_ln_b), a_blk, b_blk).reshape(t, -1)

    wr = jnp.pad(w_router, ((0, 0), (0, LANES - N_EXPERTS)))
    wr_hi = wr.astype(BF16)
    wr_lo = (wr - wr_hi.astype(F32)).astype(BF16)
    br = jnp.pad(b_router, (0, LANES - N_EXPERTS)).reshape(1, LANES)
    x1, h2, ri, rw, cnt = _mix_route(
        gret, cact, u, x2, w_ret_out.astype(BF16), w_conv_out.astype(BF16), row(b_conv_out),
        w_o.astype(BF16), row(g_ffn), wr_hi, wr_lo, br, mret_blk, mconv_blk)

    dest, block_e, n_used, cap = _route_plan(ri, cnt, t * TOP_K)
    xb = _dispatch(dest.reshape(t // TM_DISP, 1, TM_DISP * TOP_K), h2, cap)
    yb = _experts(block_e, n_used, xb, w_gu.astype(BF16), b_gu[:, None, :], w_down.astype(BF16),
                  b_down[:, None, :])
    out = _combine(dest.reshape(t // TM_COMB, 1, TM_COMB * TOP_K), x1, rw, p_i.reshape(t, -1),
                   row(g_ple), w_ple_gate.astype(BF16), w_ple_proj.astype(BF16), row(g_final), yb)
    return out.reshape(b, s, d)


def kernel(x, p, g_mix, w_in, ret_gn_w, w_ret_out, w_dw, b_dw, conv_ln_w, conv_ln_b, w_conv_out,
           b_conv_out, w_o, g_ffn, w_router, b_router, w_gu, b_gu, w_down, b_down, g_ple,
           w_ple_gate, w_ple_proj, g_final):
    assert p.shape[0] == 1, "single-layer block"
    return _layer(x, p[0], g_mix[0], w_in[0], ret_gn_w[0], w_ret_out[0], w_dw[0], b_dw[0],
                  conv_ln_w[0], conv_ln_b[0], w_conv_out[0], b_conv_out[0], w_o[0], g_ffn[0],
                  w_router[0], b_router[0], w_gu[0], b_gu[0], w_down[0], b_down[0], g_ple[0],
                  w_ple_gate[0], w_ple_proj[0], g_final)
```

```python
import functools

import numpy as np
import jax
import jax.numpy as jnp
from jax import lax
from jax.experimental import pallas as pl
from jax.experimental.pallas import tpu as pltpu

F32 = jnp.float32
BF16 = jnp.bfloat16
I32 = jnp.int32

EPS = 1e-6
RET_HEADS = 4
RET_QK_DIM = 128
RET_V_DIM = 256
RET_CHUNK = 128
ROPE_BASE = 10000.0
CONV_WIDTH = 31
N_EXPERTS = 32
TOP_K = 4
SWIGLU_LIMIT = 7.0
SWIGLU_ALPHA = 1.702

LANES = 128
SUBLANES = 8
MIB = 1024 * 1024

TM_PROJ = 512
TN_PROJ = 1024
TS_RET = 512
TS_CONV = 512
CONV_ROWS = 32
HALO = 32
TM_MOE = 512
RUN_ALIGN = SUBLANES
RUN_CHUNK = 2 * RUN_ALIGN
PERM_ROWS = 256
SORT_ROWS = TM_MOE * TOP_K + N_EXPERTS * RUN_ALIGN
MOE_BLOCK = 256

assert SORT_ROWS % PERM_ROWS == 0 and MOE_BLOCK % RUN_CHUNK == 0
assert RUN_ALIGN & (RUN_ALIGN - 1) == 0 and PERM_ROWS & (PERM_ROWS - 1) == 0


def _params(vmem_mib, n_axes=1):
    return pltpu.CompilerParams(
        dimension_semantics=("arbitrary",) * n_axes,
        vmem_limit_bytes=int(vmem_mib * MIB),
    )


def _const_spec(shape):
    nd = len(shape)
    return pl.BlockSpec(shape, lambda *_: (0,) * nd)


def _sigmoid(v):
    return 1.0 / (1.0 + jnp.exp(-v))


def _rms(v, g):
    ms = jnp.mean(v * v, axis=-1, keepdims=True)
    return v * lax.rsqrt(ms + EPS) * g


def _log2(n):
    return n.bit_length() - 1


def _in_proj_kernel(x_ref, g_ref, w_ref, u_ref, *, tn):
    h = _rms(x_ref[...], g_ref[...]).astype(BF16)
    for j in range(u_ref.shape[1] // tn):
        sl = slice(j * tn, (j + 1) * tn)
        u_ref[:, sl] = jnp.dot(h, w_ref[:, sl], preferred_element_type=F32).astype(BF16)


def _in_proj(x2, g, w_bf):
    t, d = x2.shape
    n = w_bf.shape[1]
    tm = TM_PROJ
    return pl.pallas_call(
        functools.partial(_in_proj_kernel, tn=TN_PROJ),
        grid=(t // tm,),
        in_specs=[
            pl.BlockSpec((tm, d), lambda i: (i, 0)),
            _const_spec((1, d)),
            _const_spec((d, n)),
        ],
        out_specs=pl.BlockSpec((tm, n), lambda i: (i, 0)),
        out_shape=jax.ShapeDtypeStruct((t, n), BF16),
        compiler_params=_params(56),
        name="in_proj",
    )(x2, g, w_bf)


def _retention_kernel(q_ref, k_ref, v_ref, g_ref, cos_ref, sin_ref, din_ref, dq_ref, dk_ref,
                      dc_ref, gn_ref, o_ref, state_ref, *, chunk):
    @pl.when(pl.program_id(1) == 0)
    def _():
        state_ref[...] = jnp.zeros_like(state_ref)

    ts = q_ref.shape[1]
    scale = RET_QK_DIM ** -0.5

    def chunk_body(c, carry):
        r = pl.ds(pl.multiple_of(c * chunk, chunk), chunk)
        cos = cos_ref[r, :]
        sin = sin_ref[r, :]
        for h in range(RET_HEADS):
            qs = slice(h * RET_QK_DIM, (h + 1) * RET_QK_DIM)
            vs = slice(h * RET_V_DIM, (h + 1) * RET_V_DIM)
            q = q_ref[0, r, qs].astype(F32)
            k = k_ref[0, r, qs].astype(F32)
            qr = q * cos + pltpu.roll(q, RET_QK_DIM // 2, 1) * sin
            kr = (k * cos + pltpu.roll(k, RET_QK_DIM // 2, 1) * sin) * scale
            qb = qr.astype(BF16)
            v = v_ref[0, r, vs]
            scores = lax.dot_general(qb, kr.astype(BF16), (((1,), (1,)), ((), ())),
                                     preferred_element_type=F32) * din_ref[h]
            inner = jnp.dot(scores.astype(BF16), v, preferred_element_type=F32)
            st = state_ref[h]
            cross = jnp.dot(qb, st.astype(BF16), preferred_element_type=F32) * dq_ref[h]
            kd = (kr * dk_ref[h]).astype(BF16)
            state_ref[h] = st * dc_ref[h] + lax.dot_general(
                kd, v, (((0,), (0,)), ((), ())), preferred_element_type=F32)
            o = inner + cross
            mu = jnp.mean(o, axis=-1, keepdims=True)
            oc = o - mu
            var = jnp.mean(oc * oc, axis=-1, keepdims=True)
            y = oc * lax.rsqrt(var + EPS) * gn_ref[:, vs]
            gate = g_ref[0, r, vs].astype(F32)
            o_ref[0, r, vs] = (gate * _sigmoid(gate) * y).astype(BF16)
        return carry

    lax.fori_loop(0, ts // chunk, chunk_body, 0)


def _retention_tables(s):
    half = RET_QK_DIM // 2
    pos = jnp.arange(s, dtype=F32)
    inv_freq = ROPE_BASE ** (-jnp.arange(half, dtype=F32) / half)
    ang = pos[:, None] * inv_freq[None, :]
    cos = jnp.cos(ang)
    sin = jnp.sin(ang)
    cos2 = jnp.concatenate([cos, cos], axis=-1)
    sin2 = jnp.concatenate([-sin, sin], axis=-1)
    c = RET_CHUNK
    h = RET_HEADS
    log_gamma = jnp.log1p(-jnp.exp2(-5.0 - jnp.arange(h, dtype=F32)))
    idx = jnp.arange(c, dtype=F32)
    rel = idx[:, None] - idx[None, :]
    din = jnp.where(rel >= 0, jnp.exp(log_gamma[:, None, None] * jnp.maximum(rel, 0.0)), 0.0)
    dq = jnp.exp(log_gamma[:, None] * (idx + 1.0))
    dk = jnp.exp(log_gamma[:, None] * (c - 1.0 - idx))
    dc = jnp.exp(log_gamma * c)
    dq_b = jnp.broadcast_to(dq[:, :, None], (h, c, RET_V_DIM))
    dk_b = jnp.broadcast_to(dk[:, :, None], (h, c, RET_QK_DIM))
    dc_b = jnp.broadcast_to(dc[:, None, None], (h, RET_QK_DIM, RET_V_DIM))
    return cos2, sin2, din, dq_b, dk_b, dc_b


def _retention(u3, gn_w):
    b, s, _ = u3.shape
    ts = TS_RET
    qk_w = RET_HEADS * RET_QK_DIM
    v_w = RET_HEADS * RET_V_DIM
    cos2, sin2, din, dq_b, dk_b, dc_b = _retention_tables(s)
    q_blk, k_blk = 0, 1
    v_blk = (2 * qk_w) // v_w
    g_blk = v_blk + 1
    seq_spec = lambda w, blk: pl.BlockSpec((1, ts, w), lambda i, j: (i, j, blk))
    return pl.pallas_call(
        functools.partial(_retention_kernel, chunk=RET_CHUNK),
        grid=(b, s // ts),
        in_specs=[
            seq_spec(qk_w, q_blk), seq_spec(qk_w, k_blk), seq_spec(v_w, v_blk), seq_spec(v_w, g_blk),
            pl.BlockSpec((ts, RET_QK_DIM), lambda i, j: (j, 0)),
            pl.BlockSpec((ts, RET_QK_DIM), lambda i, j: (j, 0)),
            _const_spec(din.shape), _const_spec(dq_b.shape), _const_spec(dk_b.shape),
            _const_spec(dc_b.shape), _const_spec((1, v_w)),
        ],
        out_specs=pl.BlockSpec((1, ts, v_w), lambda i, j: (i, j, 0)),
        out_shape=jax.ShapeDtypeStruct((b, s, v_w), BF16),
        scratch_shapes=[pltpu.VMEM((RET_HEADS, RET_QK_DIM, RET_V_DIM), F32)],
        compiler_params=_params(32, 2),
        name="retention",
    )(u3, u3, u3, u3, cos2, sin2, din, dq_b, dk_b, dc_b, gn_w)


def _conv_kernel(a_ref, b_ref, w_ref, bias_ref, lnw_ref, lnb_ref, o_ref, ext_ref, conv_ref):
    ts = a_ref.shape[1]
    ch = a_ref.shape[2]
    first = HALO - (CONV_WIDTH - 1)
    win_rows = CONV_ROWS + HALO

    @pl.when(pl.program_id(1) == 0)
    def _():
        ext_ref[0:HALO, :] = jnp.zeros((HALO, ch), F32)

    @pl.when(pl.program_id(1) > 0)
    def _():
        ext_ref[0:HALO, :] = ext_ref[ts:ts + HALO, :]

    ext_ref[HALO:HALO + ts, :] = a_ref[0].astype(F32) * _sigmoid(b_ref[0].astype(F32))

    for l in range(ch // LANES):
        ls = slice(l * LANES, (l + 1) * LANES)
        taps = [w_ref[j:j + 1, ls] for j in range(CONV_WIDTH)]
        bias = bias_ref[:, ls]

        def row_body(i, carry):
            r0 = pl.multiple_of(i * CONV_ROWS, CONV_ROWS)
            win = ext_ref[pl.ds(r0, win_rows), ls]
            acc = jnp.broadcast_to(bias, (CONV_ROWS, LANES))
            for r in range(SUBLANES):
                sh = win if r == 0 else pltpu.roll(win, win_rows - r, 0)
                for o in range(first, first + CONV_WIDTH):
                    if o % SUBLANES == r:
                        acc = acc + taps[o - first] * sh[o - r:o - r + CONV_ROWS]
            conv_ref[pl.ds(r0, CONV_ROWS), ls] = acc
            return carry

        lax.fori_loop(0, ts // CONV_ROWS, row_body, 0)

    c = conv_ref[...]
    mu = jnp.mean(c, axis=-1, keepdims=True)
    cc = c - mu
    var = jnp.mean(cc * cc, axis=-1, keepdims=True)
    y = cc * lax.rsqrt(var + EPS) * lnw_ref[...] + lnb_ref[...]
    o_ref[0] = (y * _sigmoid(y)).astype(BF16)


def _conv_branch(u3, w_dw, b_dw, ln_w, ln_b, a_blk, b_blk):
    b, s, _ = u3.shape
    ch = w_dw.shape[1]
    ts = TS_CONV
    seq_spec = lambda blk: pl.BlockSpec((1, ts, ch), lambda i, j: (i, j, blk))
    return pl.pallas_call(
        _conv_kernel,
        grid=(b, s // ts),
        in_specs=[
            seq_spec(a_blk), seq_spec(b_blk),
            _const_spec((CONV_WIDTH, ch)), _const_spec((1, ch)), _const_spec((1, ch)), _const_spec((1, ch)),
        ],
        out_specs=pl.BlockSpec((1, ts, ch), lambda i, j: (i, j, 0)),
        out_shape=jax.ShapeDtypeStruct((b, s, ch), BF16),
        scratch_shapes=[pltpu.VMEM((ts + HALO, ch), F32), pltpu.VMEM((ts, ch), F32)],
        compiler_params=_params(32, 2),
        name="conv_branch",
    )(u3, u3, w_dw, b_dw, ln_w, ln_b)


def _mix_route_kernel(gret_ref, cact_ref, mret_ref, mconv_ref, x_ref, wro_ref, wco_ref, bco_ref,
                      wo_ref, gffn_ref, wrh_ref, wrl_ref, br_ref, upper_ref, lower_ref,
                      x1_ref, h2_ref, rrow_ref, rcol_ref, cnt_ref):
    tm = x_ref.shape[0]
    y_ret = jnp.dot(gret_ref[...], wro_ref[...], preferred_element_type=F32)
    y_conv = jnp.dot(cact_ref[...], wco_ref[...], preferred_element_type=F32) + bco_ref[...]
    mixed = (_sigmoid(mret_ref[...].astype(F32)) * y_ret
             + _sigmoid(mconv_ref[...].astype(F32)) * y_conv)
    x1 = x_ref[...] + jnp.dot(mixed.astype(BF16), wo_ref[...], preferred_element_type=F32)
    x1_ref[...] = x1
    h2 = _rms(x1, gffn_ref[...])
    hh = h2.astype(BF16)
    h2_ref[...] = hh

    hl = (h2 - hh.astype(F32)).astype(BF16)
    nt = (((1,), (1,)), ((), ()))
    logits = (lax.dot_general(wrh_ref[...], hh, nt, preferred_element_type=F32)
              + lax.dot_general(wrl_ref[...], hh, nt, preferred_element_type=F32)
              + lax.dot_general(wrh_ref[...], hl, nt, preferred_element_type=F32)) + br_ref[...]
    erow = lax.broadcasted_iota(I32, (N_EXPERTS, tm), 0)

    vals = logits
    idxs, tops = [], []
    for _ in range(TOP_K):
        m = jnp.max(vals, axis=0, keepdims=True)
        i = jnp.min(jnp.where(vals == m, erow, N_EXPERTS), axis=0, keepdims=True)
        idxs.append(i)
        tops.append(m)
        vals = jnp.where(erow == i, -jnp.inf, vals)
    exps = [jnp.exp(m - tops[0]) for m in tops]
    denom = exps[0] + exps[1] + exps[2] + exps[3]
    wts = [e / denom for e in exps]

    hits = [erow == i for i in idxs]
    onehot = jnp.zeros((N_EXPERTS, tm), F32)
    for hit in hits:
        onehot = jnp.where(hit, 1.0, onehot)
    before = jnp.dot(onehot.astype(BF16), upper_ref[...], preferred_element_type=F32)
    n_e = jnp.sum(onehot, axis=1, keepdims=True).astype(I32)
    n_pad = (n_e + (RUN_ALIGN - 1)) & -RUN_ALIGN
    off = jnp.dot(lower_ref[...], jnp.broadcast_to(n_pad.astype(F32), (N_EXPERTS, LANES)).astype(BF16),
                  preferred_element_type=F32)[:, 0:1]
    slot = before + off
    poss = [jnp.sum(jnp.where(hit, slot, 0.0), axis=0, keepdims=True) for hit in hits]
    cnt_ref[0] = n_e

    r8 = lax.broadcasted_iota(I32, (SUBLANES, tm), 0)
    rrow = jnp.zeros((SUBLANES, tm), I32)
    rl = lax.broadcasted_iota(I32, (LANES, tm), 0)
    rcol_t = jnp.zeros((LANES, tm), F32)
    for k in range(TOP_K):
        rrow = jnp.where(r8 == k, poss[k].astype(I32), rrow)
        rcol_t = jnp.where(rl == k, poss[k], rcol_t)
        rcol_t = jnp.where(rl == TOP_K + k, wts[k], rcol_t)
    rrow_ref[...] = rrow
    rcol_ref[...] = rcol_t.T


def _mix_route(gret, cact, u, x2, wro, wco, bco, wo, gffn, wr_hi, wr_lo, br, mret_blk, mconv_blk):
    t, d = x2.shape
    tm = TM_MOE
    upper = jnp.asarray(np.triu(np.ones((tm, tm), np.float32), 1), BF16)
    lower = jnp.asarray(np.tril(np.ones((N_EXPERTS, N_EXPERTS), np.float32), -1), BF16)
    row = lambda w: pl.BlockSpec((tm, w), lambda i: (i, 0))
    col = lambda blk: pl.BlockSpec((tm, d), lambda i: (i, blk))
    return pl.pallas_call(
        _mix_route_kernel,
        grid=(t // tm,),
        in_specs=[
            row(d), row(d), col(mret_blk), col(mconv_blk), row(d),
            _const_spec((d, d)), _const_spec((d, d)), _const_spec((1, d)), _const_spec((d, d)),
            _const_spec((1, d)), _const_spec((N_EXPERTS, d)), _const_spec((N_EXPERTS, d)),
            _const_spec((N_EXPERTS, 1)), _const_spec((tm, tm)), _const_spec((N_EXPERTS, N_EXPERTS)),
        ],
        out_specs=[
            row(d), row(d),
            pl.BlockSpec((SUBLANES, tm), lambda i: (0, i)),
            row(LANES),
            pl.BlockSpec((1, N_EXPERTS, 1), lambda i: (i, 0, 0)),
        ],
        out_shape=[
            jax.ShapeDtypeStruct((t, d), F32), jax.ShapeDtypeStruct((t, d), BF16),
            jax.ShapeDtypeStruct((SUBLANES, t), I32), jax.ShapeDtypeStruct((t, LANES), F32),
            jax.ShapeDtypeStruct((t // tm, N_EXPERTS, 1), I32),
        ],
        compiler_params=_params(48),
        name="mix_route",
    )(gret, cact, u, u, x2, wro, wco, bco, wo, gffn, wr_hi, wr_lo, br, upper, lower)


def _moe_capacity(n_assign, n_tiles):
    worst = n_assign + n_tiles * N_EXPERTS * (RUN_ALIGN - 1) + N_EXPERTS * (MOE_BLOCK - 1)
    return -(-worst // MOE_BLOCK) * MOE_BLOCK


def _route_plan(cnt, n_assign):
    blk, a = MOE_BLOCK, RUN_ALIGN
    cnt_te = cnt[:, :, 0]
    rows_te = ((cnt_te + (a - 1)) // a) * a
    used = jnp.sum(rows_te, axis=0)
    padded = ((used + blk - 1) // blk) * blk
    pend = jnp.cumsum(padded)
    pstart = pend - padded
    gstart = pstart[None, :] + jnp.cumsum(rows_te, axis=0) - rows_te
    off = jnp.cumsum(rows_te, axis=1) - rows_te
    cap = _moe_capacity(n_assign, cnt_te.shape[0])
    starts = jnp.arange(cap // blk, dtype=I32) * blk
    block_e = jnp.minimum(jnp.sum((pend[None, :] <= starts[:, None]).astype(I32), axis=1), N_EXPERTS - 1)
    n_used = pend[-1:] // blk
    tail = jnp.stack([pend[-1], (cap - pend[-1]) // blk])
    flat = lambda v: v.reshape(-1).astype(I32)
    tables = dict(rows=flat(rows_te), off=flat(off), g=flat(gstart), zs=flat(pstart + used),
                  zr=flat(padded - used), tail=flat(tail))
    return tables, block_e.astype(I32), n_used.astype(I32), cap


def _for_run_pieces(rows, piece):
    full = lax.shift_right_logical(rows, _log2(RUN_CHUNK))

    def body(j, carry):
        piece(j * RUN_CHUNK, RUN_CHUNK)
        return carry

    lax.fori_loop(0, full, body, 0)
    rest = lax.shift_right_logical(rows & RUN_ALIGN, _log2(RUN_ALIGN))

    @pl.when(rest != 0)
    def _():
        piece(full * RUN_CHUNK, RUN_ALIGN)

    return full, rest


def _wait_pieces(make_copy, n_full, n_rest):
    lax.fori_loop(0, n_full, lambda j, c: (make_copy(RUN_CHUNK).wait(), c)[1], 0)
    lax.fori_loop(0, n_rest, lambda j, c: (make_copy(RUN_ALIGN).wait(), c)[1], 0)


def _dispatch_kernel(rows_ref, off_ref, g_ref, zs_ref, zr_ref, tail_ref, rrow_ref, h2_ref, xb_ref,
                     sorted_ref, zero_ref, sem, zsem, pending_ref):
    i = pl.program_id(0)
    last = pl.num_programs(0) - 1
    tm = h2_ref.shape[0]
    slot = lax.rem(i, 2)
    base = i * N_EXPERTS

    def zero_copy(dst, n):
        return pltpu.make_async_copy(zero_ref.at[pl.ds(0, n)], xb_ref.at[pl.ds(pl.multiple_of(dst, RUN_ALIGN), n)],
                                     zsem)

    def tail_copy(j):
        dst = pl.multiple_of(tail_ref[0] + j * MOE_BLOCK, MOE_BLOCK)
        return pltpu.make_async_copy(zero_ref, xb_ref.at[pl.ds(dst, MOE_BLOCK)], zsem)

    @pl.when(i == 0)
    def _():
        zero_ref[...] = jnp.zeros_like(zero_ref)

        def fill(e, counts):
            full, rest = _for_run_pieces(zr_ref[e], lambda o, n: zero_copy(zs_ref[e] + o, n).start())
            return counts[0] + full, counts[1] + rest

        n_full, n_rest = lax.fori_loop(0, N_EXPERTS, fill, (jnp.int32(0), jnp.int32(0)))
        lax.fori_loop(0, tail_ref[1], lambda j, c: (tail_copy(j).start(), c)[1], 0)
        _wait_pieces(lambda n: zero_copy(0, n), n_full, n_rest)
        lax.fori_loop(0, tail_ref[1], lambda j, c: (tail_copy(j).wait(), c)[1], 0)
        pending_ref[0] = 0
        pending_ref[1] = 0

    used = off_ref[base + N_EXPERTS - 1] + rows_ref[base + N_EXPERTS - 1]
    rrow = rrow_ref[...]

    def build(c, carry):
        r0 = pl.multiple_of(c * PERM_ROWS, PERM_ROWS)
        rows = lax.broadcasted_iota(I32, (PERM_ROWS, tm), 0) + r0
        perm = jnp.zeros((PERM_ROWS, tm), F32)
        for k in range(TOP_K):
            perm = jnp.where(rows == rrow[k:k + 1, :], 1.0, perm)
        sorted_ref[slot, pl.ds(r0, PERM_ROWS), :] = jnp.dot(perm.astype(BF16), h2_ref[...],
                                                            preferred_element_type=F32)
        return carry

    lax.fori_loop(0, lax.shift_right_logical(used + (PERM_ROWS - 1), _log2(PERM_ROWS)), build, 0)

    def run_copy(src, dst, n):
        return pltpu.make_async_copy(sorted_ref.at[slot, pl.ds(pl.multiple_of(src, RUN_ALIGN), n)],
                                     xb_ref.at[pl.ds(pl.multiple_of(dst, RUN_ALIGN), n)], sem)

    _wait_pieces(lambda n: run_copy(0, 0, n), pending_ref[0], pending_ref[1])

    def issue(e, counts):
        src0 = off_ref[base + e]
        dst0 = g_ref[base + e]
        full, rest = _for_run_pieces(rows_ref[base + e], lambda o, n: run_copy(src0 + o, dst0 + o, n).start())
        return counts[0] + full, counts[1] + rest

    n_full, n_rest = lax.fori_loop(0, N_EXPERTS, issue, (jnp.int32(0), jnp.int32(0)))
    pending_ref[0] = n_full
    pending_ref[1] = n_rest

    @pl.when(i == last)
    def _():
        _wait_pieces(lambda n: run_copy(0, 0, n), n_full, n_rest)


def _dispatch(tables, rrow, h2, cap):
    t, d = h2.shape
    tm = TM_MOE
    grid_spec = pltpu.PrefetchScalarGridSpec(
        num_scalar_prefetch=6,
        grid=(t // tm,),
        in_specs=[
            pl.BlockSpec((SUBLANES, tm), lambda i, *_: (0, i)),
            pl.BlockSpec((tm, d), lambda i, *_: (i, 0)),
        ],
        out_specs=pl.BlockSpec(memory_space=pl.ANY),
        scratch_shapes=[
            pltpu.VMEM((2, SORT_ROWS, d), F32),
            pltpu.VMEM((MOE_BLOCK, d), F32),
            pltpu.SemaphoreType.DMA, pltpu.SemaphoreType.DMA,
            pltpu.SMEM((2,), I32),
        ],
    )
    return pl.pallas_call(
        _dispatch_kernel,
        grid_spec=grid_spec,
        out_shape=jax.ShapeDtypeStruct((cap, d), F32),
        compiler_params=_params(40),
        name="dispatch",
    )(tables["rows"], tables["off"], tables["g"], tables["zs"], tables["zr"], tables["tail"], rrow, h2)


def _expert_kernel(be_ref, nu_ref, xb_ref, wgu_ref, bgu_ref, wd_ref, bd_ref, yb_ref, wgu_bf, wd_bf):
    b = pl.program_id(0)
    dff = wd_ref.shape[1]
    live = b < nu_ref[0]
    fresh = (b == 0) | (be_ref[b] != be_ref[jnp.maximum(b - 1, 0)])

    @pl.when(live & fresh)
    def _():
        wgu_bf[...] = wgu_ref[0].astype(BF16)
        wd_bf[...] = wd_ref[0].astype(BF16)

    @pl.when(live)
    def _():
        gu = jnp.dot(xb_ref[...].astype(BF16), wgu_bf[...], preferred_element_type=F32) + bgu_ref[0]
        gate = jnp.minimum(gu[:, :dff], SWIGLU_LIMIT)
        up = jnp.clip(gu[:, dff:], -SWIGLU_LIMIT, SWIGLU_LIMIT)
        act = (up + 1.0) * gate * _sigmoid(SWIGLU_ALPHA * gate)
        yb_ref[...] = jnp.dot(act.astype(BF16), wd_bf[...], preferred_element_type=F32) + bd_ref[0]

    @pl.when(jnp.logical_not(live))
    def _():
        yb_ref[...] = jnp.zeros_like(yb_ref)


def _experts(block_e, n_used, xb, wgu, bgu, wd, bd):
    cap, d = xb.shape
    dff2 = wgu.shape[2]
    blk = MOE_BLOCK
    grid_spec = pltpu.PrefetchScalarGridSpec(
        num_scalar_prefetch=2,
        grid=(cap // blk,),
        in_specs=[
            pl.BlockSpec((blk, d), lambda b, be, nu: (b, 0)),
            pl.BlockSpec((1, d, dff2), lambda b, be, nu: (be[b], 0, 0)),
            pl.BlockSpec((1, 1, dff2), lambda b, be, nu: (be[b], 0, 0)),
            pl.BlockSpec((1, dff2 // 2, d), lambda b, be, nu: (be[b], 0, 0)),
            pl.BlockSpec((1, 1, d), lambda b, be, nu: (be[b], 0, 0)),
        ],
        out_specs=pl.BlockSpec((blk, d), lambda b, be, nu: (b, 0)),
        scratch_shapes=[pltpu.VMEM((d, dff2), BF16), pltpu.VMEM((dff2 // 2, d), BF16)],
    )
    return pl.pallas_call(
        _expert_kernel,
        grid_spec=grid_spec,
        out_shape=jax.ShapeDtypeStruct((cap, d), F32),
        compiler_params=_params(56),
        name="experts",
    )(block_e, n_used, xb, wgu, bgu, wd, bd)


def _combine_kernel(rows_ref, off_ref, g_ref, rcol_ref, x1_ref, p_ref, gple_ref, wpg_ref, wpp_ref,
                    gfin_ref, yb_ref, out_ref, ys_ref, perm_ref, sems, counts_ref):
    i = pl.program_id(0)
    n_steps = pl.num_programs(0)
    tm = x1_ref.shape[0]
    slot = lax.rem(i, 2)

    def run_copy(s, src, dst, n):
        return pltpu.make_async_copy(yb_ref.at[pl.ds(pl.multiple_of(src, RUN_ALIGN), n)],
                                     ys_ref.at[s, pl.ds(pl.multiple_of(dst, RUN_ALIGN), n)], sems.at[s])

    def fetch(tile, s):
        base = tile * N_EXPERTS

        def issue(e, counts):
            src0 = g_ref[base + e]
            dst0 = off_ref[base + e]
            full, rest = _for_run_pieces(rows_ref[base + e],
                                         lambda o, n: run_copy(s, src0 + o, dst0 + o, n).start())
            return counts[0] + full, counts[1] + rest

        n_full, n_rest = lax.fori_loop(0, N_EXPERTS, issue, (jnp.int32(0), jnp.int32(0)))
        counts_ref[s, 0] = n_full
        counts_ref[s, 1] = n_rest

    @pl.when(i == 0)
    def _():
        ys_ref[...] = jnp.zeros_like(ys_ref)
        fetch(0, 0)

    @pl.when(i + 1 < n_steps)
    def _():
        fetch(i + 1, 1 - slot)

    rcol = rcol_ref[...]
    for c in range(SORT_ROWS // PERM_ROWS):
        cols = (lax.broadcasted_iota(I32, (tm, PERM_ROWS), 1) + c * PERM_ROWS).astype(F32)
        perm = jnp.zeros((tm, PERM_ROWS), F32)
        for k in range(TOP_K):
            perm = jnp.where(cols == rcol[:, k:k + 1], rcol[:, TOP_K + k:TOP_K + k + 1], perm)
        perm_ref[:, c * PERM_ROWS:(c + 1) * PERM_ROWS] = perm.astype(BF16)

    _wait_pieces(lambda n: run_copy(slot, 0, 0, n), counts_ref[slot, 0], counts_ref[slot, 1])

    y = jnp.dot(perm_ref[...], ys_ref[slot].astype(BF16), preferred_element_type=F32)
    x2 = x1_ref[...] + y
    h3 = _rms(x2, gple_ref[...]).astype(BF16)
    gate = _sigmoid(jnp.dot(h3, wpg_ref[...], preferred_element_type=F32))
    proj = jnp.dot(p_ref[...].astype(BF16), wpp_ref[...], preferred_element_type=F32)
    x3 = x2 + gate * proj
    out_ref[...] = _rms(x3, gfin_ref[...])


def _combine(tables, rcol, x1, p2, gple, wpg, wpp, gfin, yb):
    t, d = x1.shape
    pd = p2.shape[1]
    tm = TM_MOE
    row = lambda w: pl.BlockSpec((tm, w), lambda i, *_: (i, 0))
    const = lambda shape: pl.BlockSpec(shape, lambda i, *_: (0,) * len(shape))
    grid_spec = pltpu.PrefetchScalarGridSpec(
        num_scalar_prefetch=3,
        grid=(t // tm,),
        in_specs=[
            row(LANES), row(d), row(pd),
            const((1, d)), const((d, d)), const((pd, d)), const((1, d)),
            pl.BlockSpec(memory_space=pl.ANY),
        ],
        out_specs=row(d),
        scratch_shapes=[
            pltpu.VMEM((2, SORT_ROWS, d), F32),
            pltpu.VMEM((tm, SORT_ROWS), BF16),
            pltpu.SemaphoreType.DMA((2,)),
            pltpu.SMEM((2, 2), I32),
        ],
    )
    return pl.pallas_call(
        _combine_kernel,
        grid_spec=grid_spec,
        out_shape=jax.ShapeDtypeStruct((t, d), F32),
        compiler_params=_params(56),
        name="combine",
    )(tables["rows"], tables["off"], tables["g"], rcol, x1, p2, gple, wpg, wpp, gfin, yb)


def _layer(x, p_i, g_mix, w_in, ret_gn_w, w_ret_out, w_dw, b_dw, conv_ln_w, conv_ln_b, w_conv_out,
           b_conv_out, w_o, g_ffn, w_router, b_router, w_gu, b_gu, w_down, b_down, g_ple,
           w_ple_gate, w_ple_proj, g_final):
    b, s, d = x.shape
    t = b * s
    x2 = x.reshape(t, d)
    row = lambda v: v.reshape(1, -1)

    u = _in_proj(x2, row(g_mix), w_in.astype(BF16))
    u3 = u.reshape(b, s, -1)
    gret = _retention(u3, row(ret_gn_w)).reshape(t, -1)
    a_blk, b_blk, mret_blk, mconv_blk = 3, 4, 5, 6
    cact = _conv_branch(u3, w_dw, row(b_dw), row(conv_ln_w), row(conv_ln_b), a_blk, b_blk).reshape(t, -1)

    wr_t = w_router.T
    wr_hi = wr_t.astype(BF16)
    wr_lo = (wr_t - wr_hi.astype(F32)).astype(BF16)
    x1, h2, rrow, rcol, cnt = _mix_route(
        gret, cact, u, x2, w_ret_out.astype(BF16), w_conv_out.astype(BF16), row(b_conv_out),
        w_o.astype(BF16), row(g_ffn), wr_hi, wr_lo, b_router.reshape(-1, 1), mret_blk, mconv_blk)

    tables, block_e, n_used, cap = _route_plan(cnt, t * TOP_K)
    xb = _dispatch(tables, rrow, h2, cap)
    yb = _experts(block_e, n_used, xb, w_gu, b_gu[:, None, :], w_down, b_down[:, None, :])
    out = _combine(tables, rcol, x1, p_i.reshape(t, -1), row(g_ple), w_ple_gate.astype(BF16),
                   w_ple_proj.astype(BF16), row(g_final), yb)
    return out.reshape(b, s, d)


def kernel(x, p, g_mix, w_in, ret_gn_w, w_ret_out, w_dw, b_dw, conv_ln_w, conv_ln_b, w_conv_out,
           b_conv_out, w_o, g_ffn, w_router, b_router, w_gu, b_gu, w_down, b_down, g_ple,
           w_ple_gate, w_ple_proj, g_final):
    assert p.shape[0] == 1, "single-layer block"
    return _layer(x, p[0], g_mix[0], w_in[0], ret_gn_w[0], w_ret_out[0], w_dw[0], b_dw[0],
                  conv_ln_w[0], conv_ln_b[0], w_conv_out[0], b_conv_out[0], w_o[0], g_ffn[0],
                  w_router[0], b_router[0], w_gu[0], b_gu[0], w_down[0], b_down[0], g_ple[0],
                  w_ple_gate[0], w_ple_proj[0], g_final)
```

```python
import functools

import numpy as np
import jax
import jax.numpy as jnp
from jax import lax
from jax.experimental import pallas as pl
from jax.experimental.pallas import tpu as pltpu

F32 = jnp.float32
BF16 = jnp.bfloat16
I32 = jnp.int32

EPS = 1e-6
RET_HEADS = 4
RET_QK_DIM = 128
RET_V_DIM = 256
RET_CHUNK = 128
ROPE_BASE = 10000.0
CONV_WIDTH = 31
N_EXPERTS = 32
TOP_K = 4
SWIGLU_LIMIT = 7.0
SWIGLU_ALPHA = 1.702

LANES = 128
SUBLANES = 8
MIB = 1024 * 1024

TM_PROJ = 512
TN_PROJ = 1024
TS_RET = 512
TS_CONV = 512
CONV_ROWS = 32
HALO = 32
TM_MOE = 512
RUN_ALIGN = SUBLANES
PERM_ROWS = 256
SORT_ROWS = TM_MOE * TOP_K + N_EXPERTS * RUN_ALIGN
MOE_BLOCK = 512

assert SORT_ROWS % PERM_ROWS == 0 and MOE_BLOCK % RUN_ALIGN == 0
assert RUN_ALIGN & (RUN_ALIGN - 1) == 0
assert PERM_ROWS <= 256, "in-chunk offsets must be exact in bf16 (8 significant bits)"


def _params(vmem_mib, n_axes=1):
    return pltpu.CompilerParams(
        dimension_semantics=("arbitrary",) * n_axes,
        vmem_limit_bytes=int(vmem_mib * MIB),
    )


def _const_spec(shape):
    nd = len(shape)
    return pl.BlockSpec(shape, lambda *_: (0,) * nd)


def _sigmoid(v):
    return 1.0 / (1.0 + jnp.exp(-v))


def _rms(v, g):
    ms = jnp.mean(v * v, axis=-1, keepdims=True)
    return v * lax.rsqrt(ms + EPS) * g


def _in_proj_kernel(x_ref, g_ref, w_ref, u_ref, *, tn):
    h = _rms(x_ref[...], g_ref[...]).astype(BF16)
    for j in range(u_ref.shape[1] // tn):
        sl = slice(j * tn, (j + 1) * tn)
        u_ref[:, sl] = jnp.dot(h, w_ref[:, sl], preferred_element_type=F32).astype(BF16)


def _in_proj(x2, g, w_bf):
    t, d = x2.shape
    n = w_bf.shape[1]
    tm = TM_PROJ
    return pl.pallas_call(
        functools.partial(_in_proj_kernel, tn=TN_PROJ),
        grid=(t // tm,),
        in_specs=[
            pl.BlockSpec((tm, d), lambda i: (i, 0)),
            _const_spec((1, d)),
            _const_spec((d, n)),
        ],
        out_specs=pl.BlockSpec((tm, n), lambda i: (i, 0)),
        out_shape=jax.ShapeDtypeStruct((t, n), BF16),
        compiler_params=_params(56),
        name="in_proj",
    )(x2, g, w_bf)


def _retention_kernel(q_ref, k_ref, v_ref, g_ref, cos_ref, sin_ref, din_ref, dq_ref, dk_ref,
                      dc_ref, gn_ref, o_ref, state_ref, *, chunk):
    @pl.when(pl.program_id(1) == 0)
    def _():
        state_ref[...] = jnp.zeros_like(state_ref)

    ts = q_ref.shape[1]
    scale = RET_QK_DIM ** -0.5

    def chunk_body(c, carry):
        r = pl.ds(pl.multiple_of(c * chunk, chunk), chunk)
        cos = cos_ref[r, :]
        sin = sin_ref[r, :]
        for h in range(RET_HEADS):
            qs = slice(h * RET_QK_DIM, (h + 1) * RET_QK_DIM)
            vs = slice(h * RET_V_DIM, (h + 1) * RET_V_DIM)
            q = q_ref[0, r, qs].astype(F32)
            k = k_ref[0, r, qs].astype(F32)
            qr = q * cos + pltpu.roll(q, RET_QK_DIM // 2, 1) * sin
            kr = (k * cos + pltpu.roll(k, RET_QK_DIM // 2, 1) * sin) * scale
            qb = qr.astype(BF16)
            v = v_ref[0, r, vs]
            scores = lax.dot_general(qb, kr.astype(BF16), (((1,), (1,)), ((), ())),
                                     preferred_element_type=F32) * din_ref[h]
            inner = jnp.dot(scores.astype(BF16), v, preferred_element_type=F32)
            st = state_ref[h]
            cross = jnp.dot(qb, st.astype(BF16), preferred_element_type=F32) * dq_ref[h]
            kd = (kr * dk_ref[h]).astype(BF16)
            state_ref[h] = st * dc_ref[h] + lax.dot_general(
                kd, v, (((0,), (0,)), ((), ())), preferred_element_type=F32)
            o = inner + cross
            mu = jnp.mean(o, axis=-1, keepdims=True)
            oc = o - mu
            var = jnp.mean(oc * oc, axis=-1, keepdims=True)
            y = oc * lax.rsqrt(var + EPS) * gn_ref[:, vs]
            gate = g_ref[0, r, vs].astype(F32)
            o_ref[0, r, vs] = (gate * _sigmoid(gate) * y).astype(BF16)
        return carry

    lax.fori_loop(0, ts // chunk, chunk_body, 0)


def _retention_tables(s):
    half = RET_QK_DIM // 2
    pos = jnp.arange(s, dtype=F32)
    inv_freq = ROPE_BASE ** (-jnp.arange(half, dtype=F32) / half)
    ang = pos[:, None] * inv_freq[None, :]
    cos = jnp.cos(ang)
    sin = jnp.sin(ang)
    cos2 = jnp.concatenate([cos, cos], axis=-1)
    sin2 = jnp.concatenate([-sin, sin], axis=-1)
    c = RET_CHUNK
    h = RET_HEADS
    log_gamma = jnp.log1p(-jnp.exp2(-5.0 - jnp.arange(h, dtype=F32)))
    idx = jnp.arange(c, dtype=F32)
    rel = idx[:, None] - idx[None, :]
    din = jnp.where(rel >= 0, jnp.exp(log_gamma[:, None, None] * jnp.maximum(rel, 0.0)), 0.0)
    dq = jnp.exp(log_gamma[:, None] * (idx + 1.0))
    dk = jnp.exp(log_gamma[:, None] * (c - 1.0 - idx))
    dc = jnp.exp(log_gamma * c)
    dq_b = jnp.broadcast_to(dq[:, :, None], (h, c, RET_V_DIM))
    dk_b = jnp.broadcast_to(dk[:, :, None], (h, c, RET_QK_DIM))
    dc_b = jnp.broadcast_to(dc[:, None, None], (h, RET_QK_DIM, RET_V_DIM))
    return cos2, sin2, din, dq_b, dk_b, dc_b


def _retention(u3, gn_w):
    b, s, _ = u3.shape
    ts = TS_RET
    qk_w = RET_HEADS * RET_QK_DIM
    v_w = RET_HEADS * RET_V_DIM
    cos2, sin2, din, dq_b, dk_b, dc_b = _retention_tables(s)
    q_blk, k_blk = 0, 1
    v_blk = (2 * qk_w) // v_w
    g_blk = v_blk + 1
    seq_spec = lambda w, blk: pl.BlockSpec((1, ts, w), lambda i, j: (i, j, blk))
    return pl.pallas_call(
        functools.partial(_retention_kernel, chunk=RET_CHUNK),
        grid=(b, s // ts),
        in_specs=[
            seq_spec(qk_w, q_blk), seq_spec(qk_w, k_blk), seq_spec(v_w, v_blk), seq_spec(v_w, g_blk),
            pl.BlockSpec((ts, RET_QK_DIM), lambda i, j: (j, 0)),
            pl.BlockSpec((ts, RET_QK_DIM), lambda i, j: (j, 0)),
            _const_spec(din.shape), _const_spec(dq_b.shape), _const_spec(dk_b.shape),
            _const_spec(dc_b.shape), _const_spec((1, v_w)),
        ],
        out_specs=pl.BlockSpec((1, ts, v_w), lambda i, j: (i, j, 0)),
        out_shape=jax.ShapeDtypeStruct((b, s, v_w), BF16),
        scratch_shapes=[pltpu.VMEM((RET_HEADS, RET_QK_DIM, RET_V_DIM), F32)],
        compiler_params=_params(32, 2),
        name="retention",
    )(u3, u3, u3, u3, cos2, sin2, din, dq_b, dk_b, dc_b, gn_w)


def _conv_kernel(a_ref, b_ref, w_ref, bias_ref, lnw_ref, lnb_ref, o_ref, ext_ref, conv_ref):
    ts = a_ref.shape[1]
    ch = a_ref.shape[2]
    n_slabs = ch // LANES
    first = HALO - (CONV_WIDTH - 1)

    @pl.when(pl.program_id(1) == 0)
    def _():
        ext_ref[:, 0:HALO, :] = jnp.zeros((n_slabs, HALO, LANES), F32)

    @pl.when(pl.program_id(1) > 0)
    def _():
        ext_ref[:, 0:HALO, :] = ext_ref[:, ts:ts + HALO, :]

    glu = a_ref[0].astype(F32) * _sigmoid(b_ref[0].astype(F32))
    for l in range(n_slabs):
        ext_ref[l, HALO:HALO + ts, :] = glu[:, l * LANES:(l + 1) * LANES]

    def slab_body(l, carry):
        taps = [w_ref[l, j:j + 1, :] for j in range(CONV_WIDTH)]
        bias = jnp.broadcast_to(bias_ref[l], (CONV_ROWS, LANES))
        for r0 in range(0, ts, CONV_ROWS):
            acc = bias
            for j in range(CONV_WIDTH):
                acc = acc + taps[j] * ext_ref[l, r0 + first + j:r0 + first + j + CONV_ROWS, :]
            conv_ref[l, r0:r0 + CONV_ROWS, :] = acc
        return carry

    lax.fori_loop(0, n_slabs, slab_body, 0)

    c = jnp.concatenate([conv_ref[l] for l in range(n_slabs)], axis=1)
    mu = jnp.mean(c, axis=-1, keepdims=True)
    cc = c - mu
    var = jnp.mean(cc * cc, axis=-1, keepdims=True)
    y = cc * lax.rsqrt(var + EPS) * lnw_ref[...] + lnb_ref[...]
    o_ref[0] = (y * _sigmoid(y)).astype(BF16)


def _conv_branch(u3, w_dw, b_dw, ln_w, ln_b, a_blk, b_blk):
    b, s, _ = u3.shape
    ch = w_dw.shape[1]
    n_slabs = ch // LANES
    ts = TS_CONV
    w_slabs = w_dw.reshape(CONV_WIDTH, n_slabs, LANES).transpose(1, 0, 2)
    b_slabs = b_dw.reshape(n_slabs, 1, LANES)
    seq_spec = lambda blk: pl.BlockSpec((1, ts, ch), lambda i, j: (i, j, blk))
    return pl.pallas_call(
        _conv_kernel,
        grid=(b, s // ts),
        in_specs=[
            seq_spec(a_blk), seq_spec(b_blk),
            _const_spec(w_slabs.shape), _const_spec(b_slabs.shape), _const_spec((1, ch)), _const_spec((1, ch)),
        ],
        out_specs=pl.BlockSpec((1, ts, ch), lambda i, j: (i, j, 0)),
        out_shape=jax.ShapeDtypeStruct((b, s, ch), BF16),
        scratch_shapes=[pltpu.VMEM((n_slabs, ts + HALO, LANES), F32), pltpu.VMEM((n_slabs, ts, LANES), F32)],
        compiler_params=_params(32, 2),
        name="conv_branch",
    )(u3, u3, w_slabs, b_slabs, ln_w, ln_b)


def _mix_route_kernel(gret_ref, cact_ref, mret_ref, mconv_ref, x_ref, wro_ref, wco_ref, bco_ref,
                      wo_ref, gffn_ref, wrh_ref, wrl_ref, br_ref, upper_ref, lower_ref,
                      x1_ref, h2_ref, rrow_ref, rcol_ref, cnt_ref):
    tm = x_ref.shape[0]
    y_ret = jnp.dot(gret_ref[...], wro_ref[...], preferred_element_type=F32)
    y_conv = jnp.dot(cact_ref[...], wco_ref[...], preferred_element_type=F32) + bco_ref[...]
    mixed = (_sigmoid(mret_ref[...].astype(F32)) * y_ret
             + _sigmoid(mconv_ref[...].astype(F32)) * y_conv)
    x1 = x_ref[...] + jnp.dot(mixed.astype(BF16), wo_ref[...], preferred_element_type=F32)
    x1_ref[...] = x1
    h2 = _rms(x1, gffn_ref[...])
    hh = h2.astype(BF16)
    h2_ref[...] = hh

    hl = (h2 - hh.astype(F32)).astype(BF16)
    nt = (((1,), (1,)), ((), ()))
    logits = (lax.dot_general(wrh_ref[...], hh, nt, preferred_element_type=F32)
              + lax.dot_general(wrl_ref[...], hh, nt, preferred_element_type=F32)
              + lax.dot_general(wrh_ref[...], hl, nt, preferred_element_type=F32)) + br_ref[...]
    erow = lax.broadcasted_iota(I32, (N_EXPERTS, tm), 0)

    vals = logits
    idxs, tops = [], []
    for _ in range(TOP_K):
        m = jnp.max(vals, axis=0, keepdims=True)
        i = jnp.min(jnp.where(vals == m, erow, N_EXPERTS), axis=0, keepdims=True)
        idxs.append(i)
        tops.append(m)
        vals = jnp.where(erow == i, -jnp.inf, vals)
    exps = [jnp.exp(m - tops[0]) for m in tops]
    denom = exps[0] + exps[1] + exps[2] + exps[3]
    wts = [e / denom for e in exps]

    hits = [erow == i for i in idxs]
    onehot = jnp.zeros((N_EXPERTS, tm), F32)
    for hit in hits:
        onehot = jnp.where(hit, 1.0, onehot)
    before = jnp.dot(onehot.astype(BF16), upper_ref[...], preferred_element_type=F32)
    n_e = jnp.sum(onehot, axis=1, keepdims=True).astype(I32)
    n_pad = (n_e + (RUN_ALIGN - 1)) & -RUN_ALIGN
    off = jnp.dot(lower_ref[...], jnp.broadcast_to(n_pad.astype(F32), (N_EXPERTS, LANES)).astype(BF16),
                  preferred_element_type=F32)[:, 0:1]
    slot = before + off
    poss = [jnp.sum(jnp.where(hit, slot, 0.0), axis=0, keepdims=True) for hit in hits]
    cnt_ref[0] = n_e

    r8 = lax.broadcasted_iota(I32, (SUBLANES, tm), 0)
    rrow = jnp.zeros((SUBLANES, tm), I32)
    rl = lax.broadcasted_iota(I32, (LANES, tm), 0)
    rcol_t = jnp.zeros((LANES, tm), F32)
    for k in range(TOP_K):
        rrow = jnp.where(r8 == k, poss[k].astype(I32), rrow)
        rcol_t = jnp.where(rl == k, poss[k], rcol_t)
        rcol_t = jnp.where(rl == TOP_K + k, wts[k], rcol_t)
    rrow_ref[...] = rrow
    rcol_ref[...] = rcol_t.T


def _mix_route(gret, cact, u, x2, wro, wco, bco, wo, gffn, wr_hi, wr_lo, br, mret_blk, mconv_blk):
    t, d = x2.shape
    tm = TM_MOE
    upper = jnp.asarray(np.triu(np.ones((tm, tm), np.float32), 1), BF16)
    lower = jnp.asarray(np.tril(np.ones((N_EXPERTS, N_EXPERTS), np.float32), -1), BF16)
    row = lambda w: pl.BlockSpec((tm, w), lambda i: (i, 0))
    col = lambda blk: pl.BlockSpec((tm, d), lambda i: (i, blk))
    return pl.pallas_call(
        _mix_route_kernel,
        grid=(t // tm,),
        in_specs=[
            row(d), row(d), col(mret_blk), col(mconv_blk), row(d),
            _const_spec((d, d)), _const_spec((d, d)), _const_spec((1, d)), _const_spec((d, d)),
            _const_spec((1, d)), _const_spec((N_EXPERTS, d)), _const_spec((N_EXPERTS, d)),
            _const_spec((N_EXPERTS, 1)), _const_spec((tm, tm)), _const_spec((N_EXPERTS, N_EXPERTS)),
        ],
        out_specs=[
            row(d), row(d),
            pl.BlockSpec((SUBLANES, tm), lambda i: (0, i)),
            row(LANES),
            pl.BlockSpec((1, N_EXPERTS, 1), lambda i: (i, 0, 0)),
        ],
        out_shape=[
            jax.ShapeDtypeStruct((t, d), F32), jax.ShapeDtypeStruct((t, d), BF16),
            jax.ShapeDtypeStruct((SUBLANES, t), I32), jax.ShapeDtypeStruct((t, LANES), F32),
            jax.ShapeDtypeStruct((t // tm, N_EXPERTS, 1), I32),
        ],
        compiler_params=_params(48),
        name="mix_route",
    )(gret, cact, u, u, x2, wro, wco, bco, wo, gffn, wr_hi, wr_lo, br, upper, lower)


def _moe_capacity(n_assign, n_tiles):
    worst = n_assign + n_tiles * N_EXPERTS * (RUN_ALIGN - 1) + N_EXPERTS * (MOE_BLOCK - 1)
    return -(-worst // MOE_BLOCK) * MOE_BLOCK


def _route_plan(cnt, n_assign):
    blk, a = MOE_BLOCK, RUN_ALIGN
    cnt_te = cnt[:, :, 0]
    rows_te = ((cnt_te + (a - 1)) // a) * a
    used = jnp.sum(rows_te, axis=0)
    padded = ((used + blk - 1) // blk) * blk
    pend = jnp.cumsum(padded)
    pstart = pend - padded
    gstart = pstart[None, :] + jnp.cumsum(rows_te, axis=0) - rows_te
    off = jnp.cumsum(rows_te, axis=1) - rows_te
    cap = _moe_capacity(n_assign, cnt_te.shape[0])
    starts = jnp.arange(cap // blk, dtype=I32) * blk
    block_e = jnp.minimum(jnp.sum((pend[None, :] <= starts[:, None]).astype(I32), axis=1), N_EXPERTS - 1)
    n_used = pend[-1:] // blk
    tail = jnp.stack([pend[-1], (cap - pend[-1]) // blk])
    flat = lambda v: v.reshape(-1).astype(I32)
    tables = dict(rows=flat(rows_te), off=flat(off), g=flat(gstart), zs=flat(pstart + used),
                  zr=flat(padded - used), tail=flat(tail))
    return tables, block_e.astype(I32), n_used.astype(I32), cap


def _for_each_run(n_runs, rows_of, fn):
    def body(r, carry):
        rows = pl.multiple_of(rows_of(r), RUN_ALIGN)

        @pl.when(rows > 0)
        def _():
            fn(r, rows)

        return carry

    lax.fori_loop(0, n_runs, body, 0)


def _dispatch_kernel(rows_ref, off_ref, g_ref, zs_ref, zr_ref, tail_ref, rrow_ref, h2_ref, xb_ref,
                     sorted_ref, perm_ref, zero_ref, sem, zsem):
    i = pl.program_id(0)
    last = pl.num_programs(0) - 1
    tm = h2_ref.shape[0]
    slot = lax.rem(i, 2)

    def zero_copy(e, rows):
        dst = pl.multiple_of(zs_ref[e], RUN_ALIGN)
        return pltpu.make_async_copy(zero_ref.at[pl.ds(0, rows)], xb_ref.at[pl.ds(dst, rows)], zsem)

    def tail_copy(j):
        dst = pl.multiple_of(tail_ref[0] + j * MOE_BLOCK, MOE_BLOCK)
        return pltpu.make_async_copy(zero_ref, xb_ref.at[pl.ds(dst, MOE_BLOCK)], zsem)

    @pl.when(i == 0)
    def _():
        zero_ref[...] = jnp.zeros_like(zero_ref)
        pad_rows = lambda e: zr_ref[e]
        _for_each_run(N_EXPERTS, pad_rows, lambda e, rows: zero_copy(e, rows).start())
        lax.fori_loop(0, tail_ref[1], lambda j, c: (tail_copy(j).start(), c)[1], 0)
        _for_each_run(N_EXPERTS, pad_rows, lambda e, rows: zero_copy(e, rows).wait())
        lax.fori_loop(0, tail_ref[1], lambda j, c: (tail_copy(j).wait(), c)[1], 0)

    rrow = rrow_ref[...]
    chunk_rows = lax.broadcasted_iota(I32, (PERM_ROWS, tm), 0).astype(F32).astype(BF16)
    for c in range(SORT_ROWS // PERM_ROWS):
        perm = jnp.zeros((PERM_ROWS, tm), BF16)
        for k in range(TOP_K):
            rel = (rrow[k:k + 1, :] - c * PERM_ROWS).astype(F32).astype(BF16)
            perm = jnp.where(chunk_rows == rel, jnp.ones((), BF16), perm)
        perm_ref[c * PERM_ROWS:(c + 1) * PERM_ROWS, :] = perm
    sorted_ref[slot] = jnp.dot(perm_ref[...], h2_ref[...], preferred_element_type=F32)

    def run_copy(tile, e, rows):
        r = tile * N_EXPERTS + e
        src = pl.multiple_of(off_ref[r], RUN_ALIGN)
        dst = pl.multiple_of(g_ref[r], RUN_ALIGN)
        return pltpu.make_async_copy(sorted_ref.at[lax.rem(tile, 2), pl.ds(src, rows)],
                                     xb_ref.at[pl.ds(dst, rows)], sem)

    def for_runs_of(tile, fn):
        _for_each_run(N_EXPERTS, lambda e: rows_ref[tile * N_EXPERTS + e], lambda e, rows: fn(run_copy(tile, e, rows)))

    @pl.when(i > 0)
    def _():
        for_runs_of(i - 1, lambda cp: cp.wait())

    for_runs_of(i, lambda cp: cp.start())

    @pl.when(i == last)
    def _():
        for_runs_of(i, lambda cp: cp.wait())


def _dispatch(tables, rrow, h2, cap):
    t, d = h2.shape
    tm = TM_MOE
    grid_spec = pltpu.PrefetchScalarGridSpec(
        num_scalar_prefetch=6,
        grid=(t // tm,),
        in_specs=[
            pl.BlockSpec((SUBLANES, tm), lambda i, *_: (0, i)),
            pl.BlockSpec((tm, d), lambda i, *_: (i, 0)),
        ],
        out_specs=pl.BlockSpec(memory_space=pl.ANY),
        scratch_shapes=[
            pltpu.VMEM((2, SORT_ROWS, d), F32),
            pltpu.VMEM((SORT_ROWS, tm), BF16),
            pltpu.VMEM((MOE_BLOCK, d), F32),
            pltpu.SemaphoreType.DMA, pltpu.SemaphoreType.DMA,
        ],
    )
    return pl.pallas_call(
        _dispatch_kernel,
        grid_spec=grid_spec,
        out_shape=jax.ShapeDtypeStruct((cap, d), F32),
        compiler_params=_params(40),
        name="dispatch",
    )(tables["rows"], tables["off"], tables["g"], tables["zs"], tables["zr"], tables["tail"], rrow, h2)


def _expert_kernel(be_ref, nu_ref, xb_ref, wgu_ref, bgu_ref, wd_ref, bd_ref, yb_ref, wgu_bf, wd_bf):
    b = pl.program_id(0)
    dff = wd_ref.shape[1]
    live = b < nu_ref[0]
    fresh = (b == 0) | (be_ref[b] != be_ref[jnp.maximum(b - 1, 0)])

    @pl.when(live & fresh)
    def _():
        wgu_bf[...] = wgu_ref[0].astype(BF16)
        wd_bf[...] = wd_ref[0].astype(BF16)

    @pl.when(live)
    def _():
        gu = jnp.dot(xb_ref[...].astype(BF16), wgu_bf[...], preferred_element_type=F32) + bgu_ref[0]
        gate = jnp.minimum(gu[:, :dff], SWIGLU_LIMIT)
        up = jnp.clip(gu[:, dff:], -SWIGLU_LIMIT, SWIGLU_LIMIT)
        act = (up + 1.0) * gate * _sigmoid(SWIGLU_ALPHA * gate)
        yb_ref[...] = jnp.dot(act.astype(BF16), wd_bf[...], preferred_element_type=F32) + bd_ref[0]

    @pl.when(jnp.logical_not(live))
    def _():
        yb_ref[...] = jnp.zeros_like(yb_ref)


def _experts(block_e, n_used, xb, wgu, bgu, wd, bd):
    cap, d = xb.shape
    dff2 = wgu.shape[2]
    blk = MOE_BLOCK
    grid_spec = pltpu.PrefetchScalarGridSpec(
        num_scalar_prefetch=2,
        grid=(cap // blk,),
        in_specs=[
            pl.BlockSpec((blk, d), lambda b, be, nu: (b, 0)),
            pl.BlockSpec((1, d, dff2), lambda b, be, nu: (be[b], 0, 0)),
            pl.BlockSpec((1, 1, dff2), lambda b, be, nu: (be[b], 0, 0)),
            pl.BlockSpec((1, dff2 // 2, d), lambda b, be, nu: (be[b], 0, 0)),
            pl.BlockSpec((1, 1, d), lambda b, be, nu: (be[b], 0, 0)),
        ],
        out_specs=pl.BlockSpec((blk, d), lambda b, be, nu: (b, 0)),
        scratch_shapes=[pltpu.VMEM((d, dff2), BF16), pltpu.VMEM((dff2 // 2, d), BF16)],
    )
    return pl.pallas_call(
        _expert_kernel,
        grid_spec=grid_spec,
        out_shape=jax.ShapeDtypeStruct((cap, d), F32),
        compiler_params=_params(56),
        name="experts",
    )(block_e, n_used, xb, wgu, bgu, wd, bd)


def _combine_kernel(rows_ref, off_ref, g_ref, rcol_ref, x1_ref, p_ref, gple_ref, wpg_ref, wpp_ref,
                    gfin_ref, yb_ref, out_ref, ys_ref, sems):
    i = pl.program_id(0)
    n_steps = pl.num_programs(0)
    tm = x1_ref.shape[0]
    slot = lax.rem(i, 2)

    def run_copy(tile, e, rows):
        r = tile * N_EXPERTS + e
        src = pl.multiple_of(g_ref[r], RUN_ALIGN)
        dst = pl.multiple_of(off_ref[r], RUN_ALIGN)
        s = lax.rem(tile, 2)
        return pltpu.make_async_copy(yb_ref.at[pl.ds(src, rows)], ys_ref.at[s, pl.ds(dst, rows)], sems.at[s])

    def for_runs_of(tile, fn):
        _for_each_run(N_EXPERTS, lambda e: rows_ref[tile * N_EXPERTS + e], lambda e, rows: fn(run_copy(tile, e, rows)))

    @pl.when(i == 0)
    def _():
        ys_ref[...] = jnp.zeros_like(ys_ref)
        for_runs_of(0, lambda cp: cp.start())

    @pl.when(i + 1 < n_steps)
    def _():
        for_runs_of(i + 1, lambda cp: cp.start())

    for_runs_of(i, lambda cp: cp.wait())

    rcol = rcol_ref[...]
    chunk_cols = lax.broadcasted_iota(I32, (tm, PERM_ROWS), 1).astype(F32).astype(BF16)
    wts = [jnp.broadcast_to(rcol[:, TOP_K + k:TOP_K + k + 1], (tm, PERM_ROWS)).astype(BF16) for k in range(TOP_K)]
    y = jnp.zeros(x1_ref.shape, F32)
    for c in range(SORT_ROWS // PERM_ROWS):
        perm = jnp.zeros((tm, PERM_ROWS), BF16)
        for k in range(TOP_K):
            rel = jnp.broadcast_to(rcol[:, k:k + 1] - c * PERM_ROWS, (tm, PERM_ROWS)).astype(BF16)
            perm = jnp.where(chunk_cols == rel, wts[k], perm)
        rows = ys_ref[slot, c * PERM_ROWS:(c + 1) * PERM_ROWS, :].astype(BF16)
        y = y + jnp.dot(perm, rows, preferred_element_type=F32)
    x2 = x1_ref[...] + y
    h3 = _rms(x2, gple_ref[...]).astype(BF16)
    gate = _sigmoid(jnp.dot(h3, wpg_ref[...], preferred_element_type=F32))
    proj = jnp.dot(p_ref[...].astype(BF16), wpp_ref[...], preferred_element_type=F32)
    x3 = x2 + gate * proj
    out_ref[...] = _rms(x3, gfin_ref[...])


def _combine(tables, rcol, x1, p2, gple, wpg, wpp, gfin, yb):
    t, d = x1.shape
    pd = p2.shape[1]
    tm = TM_MOE
    row = lambda w: pl.BlockSpec((tm, w), lambda i, *_: (i, 0))
    const = lambda shape: pl.BlockSpec(shape, lambda i, *_: (0,) * len(shape))
    grid_spec = pltpu.PrefetchScalarGridSpec(
        num_scalar_prefetch=3,
        grid=(t // tm,),
        in_specs=[
            row(LANES), row(d), row(pd),
            const((1, d)), const((d, d)), const((pd, d)), const((1, d)),
            pl.BlockSpec(memory_space=pl.ANY),
        ],
        out_specs=row(d),
        scratch_shapes=[
            pltpu.VMEM((2, SORT_ROWS, d), F32),
            pltpu.SemaphoreType.DMA((2,)),
        ],
    )
    return pl.pallas_call(
        _combine_kernel,
        grid_spec=grid_spec,
        out_shape=jax.ShapeDtypeStruct((t, d), F32),
        compiler_params=_params(56),
        name="combine",
    )(tables["rows"], tables["off"], tables["g"], rcol, x1, p2, gple, wpg, wpp, gfin, yb)


def _layer(x, p_i, g_mix, w_in, ret_gn_w, w_ret_out, w_dw, b_dw, conv_ln_w, conv_ln_b, w_conv_out,
           b_conv_out, w_o, g_ffn, w_router, b_router, w_gu, b_gu, w_down, b_down, g_ple,
           w_ple_gate, w_ple_proj, g_final):
    b, s, d = x.shape
    t = b * s
    x2 = x.reshape(t, d)
    row = lambda v: v.reshape(1, -1)

    u = _in_proj(x2, row(g_mix), w_in.astype(BF16))
    u3 = u.reshape(b, s, -1)
    gret = _retention(u3, row(ret_gn_w)).reshape(t, -1)
    a_blk, b_blk, mret_blk, mconv_blk = 3, 4, 5, 6
    cact = _conv_branch(u3, w_dw, row(b_dw), row(conv_ln_w), row(conv_ln_b), a_blk, b_blk).reshape(t, -1)

    wr_t = w_router.T
    wr_hi = wr_t.astype(BF16)
    wr_lo = (wr_t - wr_hi.astype(F32)).astype(BF16)
    x1, h2, rrow, rcol, cnt = _mix_route(
        gret, cact, u, x2, w_ret_out.astype(BF16), w_conv_out.astype(BF16), row(b_conv_out),
        w_o.astype(BF16), row(g_ffn), wr_hi, wr_lo, b_router.reshape(-1, 1), mret_blk, mconv_blk)

    tables, block_e, n_used, cap = _route_plan(cnt, t * TOP_K)
    xb = _dispatch(tables, rrow, h2, cap)
    yb = _experts(block_e, n_used, xb, w_gu, b_gu[:, None, :], w_down, b_down[:, None, :])
    out = _combine(tables, rcol, x1, p_i.reshape(t, -1), row(g_ple), w_ple_gate.astype(BF16),
                   w_ple_proj.astype(BF16), row(g_final), yb)
    return out.reshape(b, s, d)


def kernel(x, p, g_mix, w_in, ret_gn_w, w_ret_out, w_dw, b_dw, conv_ln_w, conv_ln_b, w_conv_out,
           b_conv_out, w_o, g_ffn, w_router, b_router, w_gu, b_gu, w_down, b_down, g_ple,
           w_ple_gate, w_ple_proj, g_final):
    assert p.shape[0] == 1, "single-layer block"
    return _layer(x, p[0], g_mix[0], w_in[0], ret_gn_w[0], w_ret_out[0], w_dw[0], b_dw[0],
                  conv_ln_w[0], conv_ln_b[0], w_conv_out[0], b_conv_out[0], w_o[0], g_ffn[0],
                  w_router[0], b_router[0], w_gu[0], b_gu[0], w_down[0], b_down[0], g_ple[0],
                  w_ple_gate[0], w_ple_proj[0], g_final)
```

```python
import functools

import numpy as np
import jax
import jax.numpy as jnp
from jax import lax
from jax.experimental import pallas as pl
from jax.experimental.pallas import tpu as pltpu

F32 = jnp.float32
BF16 = jnp.bfloat16
I32 = jnp.int32

EPS = 1e-6
RET_HEADS = 4
RET_QK_DIM = 128
RET_V_DIM = 256
RET_CHUNK = 128
ROPE_BASE = 10000.0
CONV_WIDTH = 31
N_EXPERTS = 32
TOP_K = 4
SWIGLU_LIMIT = 7.0
SWIGLU_ALPHA = 1.702

LANES = 128
SUBLANES = 8
MIB = 1024 * 1024

TM_PROJ = 512
TN_PROJ = 1024
TS_RET = 1024
TS_CONV = 512
CONV_ROWS = 32
HALO = 32
TM_MOE = 512
RUN_ALIGN = SUBLANES
PERM_ROWS = 256
SORT_ROWS = TM_MOE * TOP_K + N_EXPERTS * RUN_ALIGN
MOE_BLOCK = 512

assert SORT_ROWS % PERM_ROWS == 0 and MOE_BLOCK % RUN_ALIGN == 0
assert RUN_ALIGN & (RUN_ALIGN - 1) == 0
assert PERM_ROWS <= 256, "in-chunk offsets must be exact in bf16 (8 significant bits)"


def _params(vmem_mib, n_axes=1):
    return pltpu.CompilerParams(
        dimension_semantics=("arbitrary",) * n_axes,
        vmem_limit_bytes=int(vmem_mib * MIB),
    )


def _const_spec(shape):
    nd = len(shape)
    return pl.BlockSpec(shape, lambda *_: (0,) * nd)


def _sigmoid(v):
    return 1.0 / (1.0 + jnp.exp(-v))


def _rms(v, g):
    ms = jnp.mean(v * v, axis=-1, keepdims=True)
    return v * lax.rsqrt(ms + EPS) * g


def _in_proj_kernel(x_ref, g_ref, w_ref, u_ref, *, tn):
    h = _rms(x_ref[...], g_ref[...]).astype(BF16)
    for j in range(u_ref.shape[1] // tn):
        sl = slice(j * tn, (j + 1) * tn)
        u_ref[:, sl] = jnp.dot(h, w_ref[:, sl], preferred_element_type=F32).astype(BF16)


def _in_proj(x2, g, w_bf):
    t, d = x2.shape
    n = w_bf.shape[1]
    tm = TM_PROJ
    return pl.pallas_call(
        functools.partial(_in_proj_kernel, tn=TN_PROJ),
        grid=(t // tm,),
        in_specs=[
            pl.BlockSpec((tm, d), lambda i: (i, 0)),
            _const_spec((1, d)),
            _const_spec((d, n)),
        ],
        out_specs=pl.BlockSpec((tm, n), lambda i: (i, 0)),
        out_shape=jax.ShapeDtypeStruct((t, n), BF16),
        compiler_params=_params(56),
        name="in_proj",
    )(x2, g, w_bf)


def _retention_kernel(q_ref, k_ref, v_ref, g_ref, cos_ref, sin_ref, din_ref, dq_ref, dk_ref,
                      dc_ref, gn_ref, o_ref, state_ref, *, chunk):
    @pl.when(pl.program_id(1) == 0)
    def _():
        state_ref[...] = jnp.zeros_like(state_ref)

    ts = q_ref.shape[1]
    scale = RET_QK_DIM ** -0.5

    def chunk_body(c, carry):
        r = pl.ds(pl.multiple_of(c * chunk, chunk), chunk)
        cos = cos_ref[r, :]
        sin = sin_ref[r, :]
        for h in range(RET_HEADS):
            qs = slice(h * RET_QK_DIM, (h + 1) * RET_QK_DIM)
            vs = slice(h * RET_V_DIM, (h + 1) * RET_V_DIM)
            q = q_ref[0, r, qs].astype(F32)
            k = k_ref[0, r, qs].astype(F32)
            qr = q * cos + pltpu.roll(q, RET_QK_DIM // 2, 1) * sin
            kr = (k * cos + pltpu.roll(k, RET_QK_DIM // 2, 1) * sin) * scale
            qb = qr.astype(BF16)
            v = v_ref[0, r, vs]
            scores = lax.dot_general(qb, kr.astype(BF16), (((1,), (1,)), ((), ())),
                                     preferred_element_type=F32) * din_ref[h]
            inner = jnp.dot(scores.astype(BF16), v, preferred_element_type=F32)
            st = state_ref[h]
            cross = jnp.dot(qb, st.astype(BF16), preferred_element_type=F32) * dq_ref[h]
            kd = (kr * dk_ref[h]).astype(BF16)
            state_ref[h] = st * dc_ref[h] + lax.dot_general(
                kd, v, (((0,), (0,)), ((), ())), preferred_element_type=F32)
            o = inner + cross
            mu = jnp.mean(o, axis=-1, keepdims=True)
            oc = o - mu
            var = jnp.mean(oc * oc, axis=-1, keepdims=True)
            y = oc * lax.rsqrt(var + EPS) * gn_ref[:, vs]
            gate = g_ref[0, r, vs].astype(F32)
            o_ref[0, r, vs] = (gate * _sigmoid(gate) * y).astype(BF16)
        return carry

    lax.fori_loop(0, ts // chunk, chunk_body, 0)


def _retention_tables(s):
    half = RET_QK_DIM // 2
    pos = jnp.arange(s, dtype=F32)
    inv_freq = ROPE_BASE ** (-jnp.arange(half, dtype=F32) / half)
    ang = pos[:, None] * inv_freq[None, :]
    cos = jnp.cos(ang)
    sin = jnp.sin(ang)
    cos2 = jnp.concatenate([cos, cos], axis=-1)
    sin2 = jnp.concatenate([-sin, sin], axis=-1)
    c = RET_CHUNK
    h = RET_HEADS
    log_gamma = jnp.log1p(-jnp.exp2(-5.0 - jnp.arange(h, dtype=F32)))
    idx = jnp.arange(c, dtype=F32)
    rel = idx[:, None] - idx[None, :]
    din = jnp.where(rel >= 0, jnp.exp(log_gamma[:, None, None] * jnp.maximum(rel, 0.0)), 0.0)
    dq = jnp.exp(log_gamma[:, None] * (idx + 1.0))
    dk = jnp.exp(log_gamma[:, None] * (c - 1.0 - idx))
    dc = jnp.exp(log_gamma * c)
    dq_b = jnp.broadcast_to(dq[:, :, None], (h, c, RET_V_DIM))
    dk_b = jnp.broadcast_to(dk[:, :, None], (h, c, RET_QK_DIM))
    dc_b = jnp.broadcast_to(dc[:, None, None], (h, RET_QK_DIM, RET_V_DIM))
    return cos2, sin2, din, dq_b, dk_b, dc_b


def _retention(u3, gn_w):
    b, s, _ = u3.shape
    ts = TS_RET
    qk_w = RET_HEADS * RET_QK_DIM
    v_w = RET_HEADS * RET_V_DIM
    cos2, sin2, din, dq_b, dk_b, dc_b = _retention_tables(s)
    q_blk, k_blk = 0, 1
    v_blk = (2 * qk_w) // v_w
    g_blk = v_blk + 1
    seq_spec = lambda w, blk: pl.BlockSpec((1, ts, w), lambda i, j: (i, j, blk))
    return pl.pallas_call(
        functools.partial(_retention_kernel, chunk=RET_CHUNK),
        grid=(b, s // ts),
        in_specs=[
            seq_spec(qk_w, q_blk), seq_spec(qk_w, k_blk), seq_spec(v_w, v_blk), seq_spec(v_w, g_blk),
            pl.BlockSpec((ts, RET_QK_DIM), lambda i, j: (j, 0)),
            pl.BlockSpec((ts, RET_QK_DIM), lambda i, j: (j, 0)),
            _const_spec(din.shape), _const_spec(dq_b.shape), _const_spec(dk_b.shape),
            _const_spec(dc_b.shape), _const_spec((1, v_w)),
        ],
        out_specs=pl.BlockSpec((1, ts, v_w), lambda i, j: (i, j, 0)),
        out_shape=jax.ShapeDtypeStruct((b, s, v_w), BF16),
        scratch_shapes=[pltpu.VMEM((RET_HEADS, RET_QK_DIM, RET_V_DIM), F32)],
        compiler_params=_params(32, 2),
        name="retention",
    )(u3, u3, u3, u3, cos2, sin2, din, dq_b, dk_b, dc_b, gn_w)


def _conv_kernel(a_ref, b_ref, w_ref, bias_ref, lnw_ref, lnb_ref, o_ref, ext_ref, conv_ref):
    ts = a_ref.shape[1]
    ch = a_ref.shape[2]
    n_slabs = ch // LANES
    first = HALO - (CONV_WIDTH - 1)

    @pl.when(pl.program_id(1) == 0)
    def _():
        ext_ref[:, 0:HALO, :] = jnp.zeros((n_slabs, HALO, LANES), F32)

    @pl.when(pl.program_id(1) > 0)
    def _():
        ext_ref[:, 0:HALO, :] = ext_ref[:, ts:ts + HALO, :]

    glu = a_ref[0].astype(F32) * _sigmoid(b_ref[0].astype(F32))
    for l in range(n_slabs):
        ext_ref[l, HALO:HALO + ts, :] = glu[:, l * LANES:(l + 1) * LANES]

    def slab_body(l, carry):
        taps = [w_ref[l, j:j + 1, :] for j in range(CONV_WIDTH)]
        bias = jnp.broadcast_to(bias_ref[l], (CONV_ROWS, LANES))
        for r0 in range(0, ts, CONV_ROWS):
            acc = bias
            for j in range(CONV_WIDTH):
                acc = acc + taps[j] * ext_ref[l, r0 + first + j:r0 + first + j + CONV_ROWS, :]
            conv_ref[l, r0:r0 + CONV_ROWS, :] = acc
        return carry

    lax.fori_loop(0, n_slabs, slab_body, 0)

    c = jnp.concatenate([conv_ref[l] for l in range(n_slabs)], axis=1)
    mu = jnp.mean(c, axis=-1, keepdims=True)
    cc = c - mu
    var = jnp.mean(cc * cc, axis=-1, keepdims=True)
    y = cc * lax.rsqrt(var + EPS) * lnw_ref[...] + lnb_ref[...]
    o_ref[0] = (y * _sigmoid(y)).astype(BF16)


def _conv_branch(u3, w_dw, b_dw, ln_w, ln_b, a_blk, b_blk):
    b, s, _ = u3.shape
    ch = w_dw.shape[1]
    n_slabs = ch // LANES
    ts = TS_CONV
    w_slabs = w_dw.reshape(CONV_WIDTH, n_slabs, LANES).transpose(1, 0, 2)
    b_slabs = b_dw.reshape(n_slabs, 1, LANES)
    seq_spec = lambda blk: pl.BlockSpec((1, ts, ch), lambda i, j: (i, j, blk))
    return pl.pallas_call(
        _conv_kernel,
        grid=(b, s // ts),
        in_specs=[
            seq_spec(a_blk), seq_spec(b_blk),
            _const_spec(w_slabs.shape), _const_spec(b_slabs.shape), _const_spec((1, ch)), _const_spec((1, ch)),
        ],
        out_specs=pl.BlockSpec((1, ts, ch), lambda i, j: (i, j, 0)),
        out_shape=jax.ShapeDtypeStruct((b, s, ch), BF16),
        scratch_shapes=[pltpu.VMEM((n_slabs, ts + HALO, LANES), F32), pltpu.VMEM((n_slabs, ts, LANES), F32)],
        compiler_params=_params(32, 2),
        name="conv_branch",
    )(u3, u3, w_slabs, b_slabs, ln_w, ln_b)


def _mix_route_kernel(gret_ref, cact_ref, mret_ref, mconv_ref, x_ref, wro_ref, wco_ref, bco_ref,
                      wo_ref, gffn_ref, wrh_ref, wrl_ref, br_ref, upper_ref, lower_ref,
                      x1_ref, h2_ref, rrow_ref, rcol_ref, cnt_ref):
    tm = x_ref.shape[0]
    y_ret = jnp.dot(gret_ref[...], wro_ref[...], preferred_element_type=F32)
    y_conv = jnp.dot(cact_ref[...], wco_ref[...], preferred_element_type=F32) + bco_ref[...]
    mixed = (_sigmoid(mret_ref[...].astype(F32)) * y_ret
             + _sigmoid(mconv_ref[...].astype(F32)) * y_conv)
    x1 = x_ref[...] + jnp.dot(mixed.astype(BF16), wo_ref[...], preferred_element_type=F32)
    x1_ref[...] = x1
    h2 = _rms(x1, gffn_ref[...])
    hh = h2.astype(BF16)
    h2_ref[...] = hh

    hl = (h2 - hh.astype(F32)).astype(BF16)
    nt = (((1,), (1,)), ((), ()))
    logits = (lax.dot_general(wrh_ref[...], hh, nt, preferred_element_type=F32)
              + lax.dot_general(wrl_ref[...], hh, nt, preferred_element_type=F32)
              + lax.dot_general(wrh_ref[...], hl, nt, preferred_element_type=F32)) + br_ref[...]
    erow = lax.broadcasted_iota(I32, (N_EXPERTS, tm), 0)

    vals = logits
    idxs, tops = [], []
    for _ in range(TOP_K):
        m = jnp.max(vals, axis=0, keepdims=True)
        i = jnp.min(jnp.where(vals == m, erow, N_EXPERTS), axis=0, keepdims=True)
        idxs.append(i)
        tops.append(m)
        vals = jnp.where(erow == i, -jnp.inf, vals)
    exps = [jnp.exp(m - tops[0]) for m in tops]
    denom = exps[0] + exps[1] + exps[2] + exps[3]
    wts = [e / denom for e in exps]

    hits = [erow == i for i in idxs]
    onehot = jnp.zeros((N_EXPERTS, tm), F32)
    for hit in hits:
        onehot = jnp.where(hit, 1.0, onehot)
    before = jnp.dot(onehot.astype(BF16), upper_ref[...], preferred_element_type=F32)
    n_e = jnp.sum(onehot, axis=1, keepdims=True).astype(I32)
    n_pad = (n_e + (RUN_ALIGN - 1)) & -RUN_ALIGN
    off = jnp.dot(lower_ref[...], jnp.broadcast_to(n_pad.astype(F32), (N_EXPERTS, LANES)).astype(BF16),
                  preferred_element_type=F32)[:, 0:1]
    slot = before + off
    poss = [jnp.sum(jnp.where(hit, slot, 0.0), axis=0, keepdims=True) for hit in hits]
    cnt_ref[0] = n_e

    r8 = lax.broadcasted_iota(I32, (SUBLANES, tm), 0)
    rrow = jnp.zeros((SUBLANES, tm), I32)
    rl = lax.broadcasted_iota(I32, (LANES, tm), 0)
    rcol_t = jnp.zeros((LANES, tm), F32)
    for k in range(TOP_K):
        rrow = jnp.where(r8 == k, poss[k].astype(I32), rrow)
        rcol_t = jnp.where(rl == k, poss[k], rcol_t)
        rcol_t = jnp.where(rl == TOP_K + k, wts[k], rcol_t)
    rrow_ref[...] = rrow
    rcol_ref[...] = rcol_t.T


def _mix_route(gret, cact, u, x2, wro, wco, bco, wo, gffn, wr_hi, wr_lo, br, mret_blk, mconv_blk):
    t, d = x2.shape
    tm = TM_MOE
    upper = jnp.asarray(np.triu(np.ones((tm, tm), np.float32), 1), BF16)
    lower = jnp.asarray(np.tril(np.ones((N_EXPERTS, N_EXPERTS), np.float32), -1), BF16)
    row = lambda w: pl.BlockSpec((tm, w), lambda i: (i, 0))
    col = lambda blk: pl.BlockSpec((tm, d), lambda i: (i, blk))
    return pl.pallas_call(
        _mix_route_kernel,
        grid=(t // tm,),
        in_specs=[
            row(d), row(d), col(mret_blk), col(mconv_blk), row(d),
            _const_spec((d, d)), _const_spec((d, d)), _const_spec((1, d)), _const_spec((d, d)),
            _const_spec((1, d)), _const_spec((N_EXPERTS, d)), _const_spec((N_EXPERTS, d)),
            _const_spec((N_EXPERTS, 1)), _const_spec((tm, tm)), _const_spec((N_EXPERTS, N_EXPERTS)),
        ],
        out_specs=[
            row(d), row(d),
            pl.BlockSpec((SUBLANES, tm), lambda i: (0, i)),
            row(LANES),
            pl.BlockSpec((1, N_EXPERTS, 1), lambda i: (i, 0, 0)),
        ],
        out_shape=[
            jax.ShapeDtypeStruct((t, d), F32), jax.ShapeDtypeStruct((t, d), BF16),
            jax.ShapeDtypeStruct((SUBLANES, t), I32), jax.ShapeDtypeStruct((t, LANES), F32),
            jax.ShapeDtypeStruct((t // tm, N_EXPERTS, 1), I32),
        ],
        compiler_params=_params(48),
        name="mix_route",
    )(gret, cact, u, u, x2, wro, wco, bco, wo, gffn, wr_hi, wr_lo, br, upper, lower)


def _moe_capacity(n_assign, n_tiles):
    worst = n_assign + n_tiles * N_EXPERTS * (RUN_ALIGN - 1) + N_EXPERTS * (MOE_BLOCK - 1)
    return -(-worst // MOE_BLOCK) * MOE_BLOCK


def _route_plan(cnt, n_assign):
    blk, a = MOE_BLOCK, RUN_ALIGN
    cnt_te = cnt[:, :, 0]
    rows_te = ((cnt_te + (a - 1)) // a) * a
    used = jnp.sum(rows_te, axis=0)
    padded = ((used + blk - 1) // blk) * blk
    pend = jnp.cumsum(padded)
    pstart = pend - padded
    gstart = pstart[None, :] + jnp.cumsum(rows_te, axis=0) - rows_te
    off = jnp.cumsum(rows_te, axis=1) - rows_te
    cap = _moe_capacity(n_assign, cnt_te.shape[0])
    tail = jnp.stack([pend[-1], (cap - pend[-1]) // blk])
    flat = lambda v: v.reshape(-1).astype(I32)
    tables = dict(rows=flat(rows_te), off=flat(off), g=flat(gstart), zs=flat(pstart + used),
                  zr=flat(padded - used), tail=flat(tail), first_block=flat(pstart // blk),
                  n_blocks=flat(padded // blk), tail_blocks=flat(tail // jnp.array([blk, 1])))
    return tables, cap


def _for_each_run(n_runs, rows_of, fn):
    def body(r, carry):
        rows = pl.multiple_of(rows_of(r), RUN_ALIGN)

        @pl.when(rows > 0)
        def _():
            fn(r, rows)

        return carry

    lax.fori_loop(0, n_runs, body, 0)


def _dispatch_kernel(rows_ref, off_ref, g_ref, zs_ref, zr_ref, tail_ref, rrow_ref, h2_ref, xb_ref,
                     sorted_ref, perm_ref, zero_ref, sem, zsem):
    i = pl.program_id(0)
    last = pl.num_programs(0) - 1
    tm = h2_ref.shape[0]
    slot = lax.rem(i, 2)

    def zero_copy(e, rows):
        dst = pl.multiple_of(zs_ref[e], RUN_ALIGN)
        return pltpu.make_async_copy(zero_ref.at[pl.ds(0, rows)], xb_ref.at[pl.ds(dst, rows)], zsem)

    def tail_copy(j):
        dst = pl.multiple_of(tail_ref[0] + j * MOE_BLOCK, MOE_BLOCK)
        return pltpu.make_async_copy(zero_ref, xb_ref.at[pl.ds(dst, MOE_BLOCK)], zsem)

    @pl.when(i == 0)
    def _():
        zero_ref[...] = jnp.zeros_like(zero_ref)
        pad_rows = lambda e: zr_ref[e]
        _for_each_run(N_EXPERTS, pad_rows, lambda e, rows: zero_copy(e, rows).start())
        lax.fori_loop(0, tail_ref[1], lambda j, c: (tail_copy(j).start(), c)[1], 0)
        _for_each_run(N_EXPERTS, pad_rows, lambda e, rows: zero_copy(e, rows).wait())
        lax.fori_loop(0, tail_ref[1], lambda j, c: (tail_copy(j).wait(), c)[1], 0)

    rrow = rrow_ref[...]
    chunk_rows = lax.broadcasted_iota(I32, (PERM_ROWS, tm), 0).astype(F32).astype(BF16)
    for c in range(SORT_ROWS // PERM_ROWS):
        perm = jnp.zeros((PERM_ROWS, tm), BF16)
        for k in range(TOP_K):
            rel = (rrow[k:k + 1, :] - c * PERM_ROWS).astype(F32).astype(BF16)
            perm = jnp.where(chunk_rows == rel, jnp.ones((), BF16), perm)
        perm_ref[c * PERM_ROWS:(c + 1) * PERM_ROWS, :] = perm
    sorted_ref[slot] = jnp.dot(perm_ref[...], h2_ref[...], preferred_element_type=F32)

    def run_copy(tile, e, rows):
        r = tile * N_EXPERTS + e
        src = pl.multiple_of(off_ref[r], RUN_ALIGN)
        dst = pl.multiple_of(g_ref[r], RUN_ALIGN)
        return pltpu.make_async_copy(sorted_ref.at[lax.rem(tile, 2), pl.ds(src, rows)],
                                     xb_ref.at[pl.ds(dst, rows)], sem)

    def for_runs_of(tile, fn):
        _for_each_run(N_EXPERTS, lambda e: rows_ref[tile * N_EXPERTS + e], lambda e, rows: fn(run_copy(tile, e, rows)))

    @pl.when(i > 0)
    def _():
        for_runs_of(i - 1, lambda cp: cp.wait())

    for_runs_of(i, lambda cp: cp.start())

    @pl.when(i == last)
    def _():
        for_runs_of(i, lambda cp: cp.wait())


def _dispatch(tables, rrow, h2, cap):
    t, d = h2.shape
    tm = TM_MOE
    grid_spec = pltpu.PrefetchScalarGridSpec(
        num_scalar_prefetch=6,
        grid=(t // tm,),
        in_specs=[
            pl.BlockSpec((SUBLANES, tm), lambda i, *_: (0, i)),
            pl.BlockSpec((tm, d), lambda i, *_: (i, 0)),
        ],
        out_specs=pl.BlockSpec(memory_space=pl.ANY),
        scratch_shapes=[
            pltpu.VMEM((2, SORT_ROWS, d), F32),
            pltpu.VMEM((SORT_ROWS, tm), BF16),
            pltpu.VMEM((MOE_BLOCK, d), F32),
            pltpu.SemaphoreType.DMA, pltpu.SemaphoreType.DMA,
        ],
    )
    return pl.pallas_call(
        _dispatch_kernel,
        grid_spec=grid_spec,
        out_shape=jax.ShapeDtypeStruct((cap, d), F32),
        compiler_params=_params(40),
        name="dispatch",
    )(tables["rows"], tables["off"], tables["g"], tables["zs"], tables["zr"], tables["tail"], rrow, h2)


def _expert_kernel(fb_ref, nb_ref, tail_ref, xb_ref, wgu_ref, bgu_ref, wd_ref, bd_ref, yb_ref,
                   wgu_bf, wd_bf, xbuf, ybuf, in_sems, out_sems):
    e = pl.program_id(0)
    dff = wd_ref.shape[1]
    first = fb_ref[e]
    n_blocks = nb_ref[e]

    def rows_of(j):
        return pl.ds(pl.multiple_of((first + j) * MOE_BLOCK, MOE_BLOCK), MOE_BLOCK)

    def x_copy(j, slot):
        return pltpu.make_async_copy(xb_ref.at[rows_of(j)], xbuf.at[slot], in_sems.at[slot])

    def y_copy(j, slot):
        return pltpu.make_async_copy(ybuf.at[slot], yb_ref.at[rows_of(j)], out_sems.at[slot])

    @pl.when(n_blocks > 0)
    def _():
        x_copy(0, 0).start()
        wgu_bf[...] = wgu_ref[0].astype(BF16)
        wd_bf[...] = wd_ref[0].astype(BF16)

        def block_step(j, carry):
            slot = lax.rem(j, 2)

            @pl.when(j + 1 < n_blocks)
            def _():
                x_copy(j + 1, 1 - slot).start()

            x_copy(j, slot).wait()

            @pl.when(j >= 2)
            def _():
                y_copy(j - 2, slot).wait()

            gu = jnp.dot(xbuf[slot].astype(BF16), wgu_bf[...], preferred_element_type=F32) + bgu_ref[0]
            gate = jnp.minimum(gu[:, :dff], SWIGLU_LIMIT)
            up = jnp.clip(gu[:, dff:], -SWIGLU_LIMIT, SWIGLU_LIMIT)
            act = (up + 1.0) * gate * _sigmoid(SWIGLU_ALPHA * gate)
            ybuf[slot] = jnp.dot(act.astype(BF16), wd_bf[...], preferred_element_type=F32) + bd_ref[0]
            y_copy(j, slot).start()
            return carry

        lax.fori_loop(0, n_blocks, block_step, 0)

        @pl.when(n_blocks >= 2)
        def _():
            y_copy(n_blocks - 2, lax.rem(n_blocks, 2)).wait()

        y_copy(n_blocks - 1, lax.rem(n_blocks - 1, 2)).wait()

    @pl.when(e == pl.num_programs(0) - 1)
    def _():
        ybuf[0] = jnp.zeros(ybuf.shape[1:], F32)

        def tail_copy(j):
            dst = pl.multiple_of((tail_ref[0] + j) * MOE_BLOCK, MOE_BLOCK)
            return pltpu.make_async_copy(ybuf.at[0], yb_ref.at[pl.ds(dst, MOE_BLOCK)], out_sems.at[0])

        lax.fori_loop(0, tail_ref[1], lambda j, c: (tail_copy(j).start(), c)[1], 0)
        lax.fori_loop(0, tail_ref[1], lambda j, c: (tail_copy(j).wait(), c)[1], 0)


def _experts(first_block, n_blocks, tail, xb, wgu, bgu, wd, bd):
    cap, d = xb.shape
    n_experts, _, dff2 = wgu.shape
    of_expert = lambda shape: pl.BlockSpec((1,) + shape, lambda e, *_: (e, 0, 0))
    grid_spec = pltpu.PrefetchScalarGridSpec(
        num_scalar_prefetch=3,
        grid=(n_experts,),
        in_specs=[
            pl.BlockSpec(memory_space=pl.ANY),
            of_expert((d, dff2)), of_expert((1, dff2)), of_expert((dff2 // 2, d)), of_expert((1, d)),
        ],
        out_specs=pl.BlockSpec(memory_space=pl.ANY),
        scratch_shapes=[
            pltpu.VMEM((d, dff2), BF16), pltpu.VMEM((dff2 // 2, d), BF16),
            pltpu.VMEM((2, MOE_BLOCK, d), F32), pltpu.VMEM((2, MOE_BLOCK, d), F32),
            pltpu.SemaphoreType.DMA((2,)), pltpu.SemaphoreType.DMA((2,)),
        ],
    )
    return pl.pallas_call(
        _expert_kernel,
        grid_spec=grid_spec,
        out_shape=jax.ShapeDtypeStruct((cap, d), F32),
        compiler_params=_params(56),
        name="experts",
    )(first_block, n_blocks, tail, xb, wgu, bgu, wd, bd)


def _combine_kernel(rows_ref, off_ref, g_ref, rcol_ref, x1_ref, p_ref, gple_ref, wpg_ref, wpp_ref,
                    gfin_ref, yb_ref, out_ref, ys_ref, sems):
    i = pl.program_id(0)
    n_steps = pl.num_programs(0)
    tm = x1_ref.shape[0]
    slot = lax.rem(i, 2)

    def run_copy(tile, e, rows):
        r = tile * N_EXPERTS + e
        src = pl.multiple_of(g_ref[r], RUN_ALIGN)
        dst = pl.multiple_of(off_ref[r], RUN_ALIGN)
        s = lax.rem(tile, 2)
        return pltpu.make_async_copy(yb_ref.at[pl.ds(src, rows)], ys_ref.at[s, pl.ds(dst, rows)], sems.at[s])

    def for_runs_of(tile, fn):
        _for_each_run(N_EXPERTS, lambda e: rows_ref[tile * N_EXPERTS + e], lambda e, rows: fn(run_copy(tile, e, rows)))

    @pl.when(i == 0)
    def _():
        ys_ref[...] = jnp.zeros_like(ys_ref)
        for_runs_of(0, lambda cp: cp.start())

    @pl.when(i + 1 < n_steps)
    def _():
        for_runs_of(i + 1, lambda cp: cp.start())

    for_runs_of(i, lambda cp: cp.wait())

    rcol = rcol_ref[...]
    chunk_cols = lax.broadcasted_iota(I32, (tm, PERM_ROWS), 1).astype(F32).astype(BF16)
    wts = [jnp.broadcast_to(rcol[:, TOP_K + k:TOP_K + k + 1], (tm, PERM_ROWS)).astype(BF16) for k in range(TOP_K)]
    y = jnp.zeros(x1_ref.shape, F32)
    for c in range(SORT_ROWS // PERM_ROWS):
        perm = jnp.zeros((tm, PERM_ROWS), BF16)
        for k in range(TOP_K):
            rel = jnp.broadcast_to(rcol[:, k:k + 1] - c * PERM_ROWS, (tm, PERM_ROWS)).astype(BF16)
            perm = jnp.where(chunk_cols == rel, wts[k], perm)
        rows = ys_ref[slot, c * PERM_ROWS:(c + 1) * PERM_ROWS, :].astype(BF16)
        y = y + jnp.dot(perm, rows, preferred_element_type=F32)
    x2 = x1_ref[...] + y
    h3 = _rms(x2, gple_ref[...]).astype(BF16)
    gate = _sigmoid(jnp.dot(h3, wpg_ref[...], preferred_element_type=F32))
    proj = jnp.dot(p_ref[...].astype(BF16), wpp_ref[...], preferred_element_type=F32)
    x3 = x2 + gate * proj
    out_ref[...] = _rms(x3, gfin_ref[...])


def _combine(tables, rcol, x1, p2, gple, wpg, wpp, gfin, yb):
    t, d = x1.shape
    pd = p2.shape[1]
    tm = TM_MOE
    row = lambda w: pl.BlockSpec((tm, w), lambda i, *_: (i, 0))
    const = lambda shape: pl.BlockSpec(shape, lambda i, *_: (0,) * len(shape))
    grid_spec = pltpu.PrefetchScalarGridSpec(
        num_scalar_prefetch=3,
        grid=(t // tm,),
        in_specs=[
            row(LANES), row(d), row(pd),
            const((1, d)), const((d, d)), const((pd, d)), const((1, d)),
            pl.BlockSpec(memory_space=pl.ANY),
        ],
        out_specs=row(d),
        scratch_shapes=[
            pltpu.VMEM((2, SORT_ROWS, d), F32),
            pltpu.SemaphoreType.DMA((2,)),
        ],
    )
    return pl.pallas_call(
        _combine_kernel,
        grid_spec=grid_spec,
        out_shape=jax.ShapeDtypeStruct((t, d), F32),
        compiler_params=_params(56),
        name="combine",
    )(tables["rows"], tables["off"], tables["g"], rcol, x1, p2, gple, wpg, wpp, gfin, yb)


def _layer(x, p_i, g_mix, w_in, ret_gn_w, w_ret_out, w_dw, b_dw, conv_ln_w, conv_ln_b, w_conv_out,
           b_conv_out, w_o, g_ffn, w_router, b_router, w_gu, b_gu, w_down, b_down, g_ple,
           w_ple_gate, w_ple_proj, g_final):
    b, s, d = x.shape
    t = b * s
    x2 = x.reshape(t, d)
    row = lambda v: v.reshape(1, -1)

    u = _in_proj(x2, row(g_mix), w_in.astype(BF16))
    u3 = u.reshape(b, s, -1)
    gret = _retention(u3, row(ret_gn_w)).reshape(t, -1)
    a_blk, b_blk, mret_blk, mconv_blk = 3, 4, 5, 6
    cact = _conv_branch(u3, w_dw, row(b_dw), row(conv_ln_w), row(conv_ln_b), a_blk, b_blk).reshape(t, -1)

    wr_t = w_router.T
    wr_hi = wr_t.astype(BF16)
    wr_lo = (wr_t - wr_hi.astype(F32)).astype(BF16)
    x1, h2, rrow, rcol, cnt = _mix_route(
        gret, cact, u, x2, w_ret_out.astype(BF16), w_conv_out.astype(BF16), row(b_conv_out),
        w_o.astype(BF16), row(g_ffn), wr_hi, wr_lo, b_router.reshape(-1, 1), mret_blk, mconv_blk)

    tables, cap = _route_plan(cnt, t * TOP_K)
    xb = _dispatch(tables, rrow, h2, cap)
    yb = _experts(tables["first_block"], tables["n_blocks"], tables["tail_blocks"], xb, w_gu, b_gu[:, None, :],
                  w_down, b_down[:, None, :])
    out = _combine(tables, rcol, x1, p_i.reshape(t, -1), row(g_ple), w_ple_gate.astype(BF16),
                   w_ple_proj.astype(BF16), row(g_final), yb)
    return out.reshape(b, s, d)


def kernel(x, p, g_mix, w_in, ret_gn_w, w_ret_out, w_dw, b_dw, conv_ln_w, conv_ln_b, w_conv_out,
           b_conv_out, w_o, g_ffn, w_router, b_router, w_gu, b_gu, w_down, b_down, g_ple,
           w_ple_gate, w_ple_proj, g_final):
    assert p.shape[0] == 1, "single-layer block"
    return _layer(x, p[0], g_mix[0], w_in[0], ret_gn_w[0], w_ret_out[0], w_dw[0], b_dw[0],
                  conv_ln_w[0], conv_ln_b[0], w_conv_out[0], b_conv_out[0], w_o[0], g_ffn[0],
                  w_router[0], b_router[0], w_gu[0], b_gu[0], w_down[0], b_down[0], g_ple[0],
                  w_ple_gate[0], w_ple_proj[0], g_final)
```

```python
import functools

import numpy as np
import jax
import jax.numpy as jnp
from jax import lax
from jax.experimental import pallas as pl
from jax.experimental.pallas import tpu as pltpu

F32 = jnp.float32
BF16 = jnp.bfloat16
I32 = jnp.int32

EPS = 1e-6
RET_HEADS = 4
RET_QK_DIM = 128
RET_V_DIM = 256
RET_CHUNK = 128
ROPE_BASE = 10000.0
CONV_WIDTH = 31
N_EXPERTS = 32
TOP_K = 4
SWIGLU_LIMIT = 7.0
SWIGLU_ALPHA = 1.702

LANES = 128
SUBLANES = 8
MIB = 1024 * 1024

TM_PROJ = 512
TS_RET = 512
TS_CONV = 512
CONV_ROWS = 32
HALO = 32
TM_MOE = 512
RUN_ALIGN = SUBLANES
PERM_ROWS = 256
SORT_ROWS = TM_MOE * TOP_K + N_EXPERTS * RUN_ALIGN
MOE_BLOCK = 512

assert SORT_ROWS % PERM_ROWS == 0 and MOE_BLOCK % RUN_ALIGN == 0
assert RUN_ALIGN & (RUN_ALIGN - 1) == 0
assert PERM_ROWS <= 256, "in-chunk offsets must be exact in bf16 (8 significant bits)"


def _params(vmem_mib, n_axes=1):
    return pltpu.CompilerParams(
        dimension_semantics=("arbitrary",) * n_axes,
        vmem_limit_bytes=int(vmem_mib * MIB),
    )


def _const_spec(shape):
    nd = len(shape)
    return pl.BlockSpec(shape, lambda *_: (0,) * nd)


def _sigmoid(v):
    return 1.0 / (1.0 + jnp.exp(-v))


def _rms(v, g):
    ms = jnp.mean(v * v, axis=-1, keepdims=True)
    return v * lax.rsqrt(ms + EPS) * g


U_QK, U_V, U_RET_GATE, U_GLU, U_MERGE_RET, U_MERGE_CONV = range(6)


def _in_proj_kernel(x_ref, g_ref, w_ref, wgu_ref, wd_ref, u_ref, wgu_bf_ref, wd_bf_ref):
    wgu_bf_ref[...] = wgu_ref[...].astype(BF16)
    wd_bf_ref[...] = wd_ref[...].astype(BF16)

    d = x_ref.shape[1]
    h = _rms(x_ref[...], g_ref[...]).astype(BF16)
    proj = lambda blk: jnp.dot(h, w_ref[:, blk * d:(blk + 1) * d], preferred_element_type=F32)

    def put(blk, val):
        u_ref[:, blk * d:(blk + 1) * d] = val.astype(BF16)

    put(U_QK, proj(0))
    put(U_V, proj(1))
    put(U_RET_GATE, proj(2))
    put(U_GLU, proj(3) * _sigmoid(proj(4)))
    put(U_MERGE_RET, proj(5))
    put(U_MERGE_CONV, proj(6))


def _in_proj(x2, g, w_bf, w_gu, w_down):
    t, d = x2.shape
    n = (U_MERGE_CONV + 1) * d
    tm = TM_PROJ
    steps = t // tm
    slices = lambda w: w.reshape(steps, -1, w.shape[-1])
    gu_s, down_s = slices(w_gu), slices(w_down)
    slice_spec = lambda w: pl.BlockSpec((1,) + w.shape[1:], lambda i: (i, 0, 0))
    u, gu_bf, down_bf = pl.pallas_call(
        _in_proj_kernel,
        grid=(steps,),
        in_specs=[
            pl.BlockSpec((tm, d), lambda i: (i, 0)),
            _const_spec((1, d)),
            pl.BlockSpec(w_bf.shape, lambda i: (0, 0), pipeline_mode=pl.Buffered(1)),
            slice_spec(gu_s), slice_spec(down_s),
        ],
        out_specs=[pl.BlockSpec((tm, n), lambda i: (i, 0)), slice_spec(gu_s), slice_spec(down_s)],
        out_shape=[jax.ShapeDtypeStruct((t, n), BF16), jax.ShapeDtypeStruct(gu_s.shape, BF16),
                   jax.ShapeDtypeStruct(down_s.shape, BF16)],
        compiler_params=_params(56),
        name="in_proj",
    )(x2, g, w_bf, gu_s, down_s)
    return u, gu_bf.reshape(w_gu.shape), down_bf.reshape(w_down.shape)


def _retention_kernel(q_ref, k_ref, v_ref, g_ref, cos_ref, sin_ref, din_ref, dq_ref, dk_ref,
                      dc_ref, gn_ref, o_ref, state_ref, *, chunk):
    @pl.when(pl.program_id(1) == 0)
    def _():
        state_ref[...] = jnp.zeros_like(state_ref)

    ts = q_ref.shape[1]
    scale = RET_QK_DIM ** -0.5

    def chunk_body(c, carry):
        r = pl.ds(pl.multiple_of(c * chunk, chunk), chunk)
        cos = cos_ref[r, :]
        sin = sin_ref[r, :]
        for h in range(RET_HEADS):
            qs = slice(h * RET_QK_DIM, (h + 1) * RET_QK_DIM)
            vs = slice(h * RET_V_DIM, (h + 1) * RET_V_DIM)
            q = q_ref[0, r, qs].astype(F32)
            k = k_ref[0, r, qs].astype(F32)
            qr = q * cos + pltpu.roll(q, RET_QK_DIM // 2, 1) * sin
            kr = (k * cos + pltpu.roll(k, RET_QK_DIM // 2, 1) * sin) * scale
            qb = qr.astype(BF16)
            v = v_ref[0, r, vs]
            scores = lax.dot_general(qb, kr.astype(BF16), (((1,), (1,)), ((), ())),
                                     preferred_element_type=F32) * din_ref[h]
            inner = jnp.dot(scores.astype(BF16), v, preferred_element_type=F32)
            st = state_ref[h]
            cross = jnp.dot(qb, st.astype(BF16), preferred_element_type=F32) * dq_ref[h]
            kd = (kr * dk_ref[h]).astype(BF16)
            state_ref[h] = st * dc_ref[h] + lax.dot_general(
                kd, v, (((0,), (0,)), ((), ())), preferred_element_type=F32)
            o = inner + cross
            mu = jnp.mean(o, axis=-1, keepdims=True)
            oc = o - mu
            var = jnp.mean(oc * oc, axis=-1, keepdims=True)
            y = oc * lax.rsqrt(var + EPS) * gn_ref[:, vs]
            gate = g_ref[0, r, vs].astype(F32)
            o_ref[0, r, vs] = (gate * _sigmoid(gate) * y).astype(BF16)
        return carry

    lax.fori_loop(0, ts // chunk, chunk_body, 0)


def _retention_tables(s):
    half = RET_QK_DIM // 2
    pos = jnp.arange(s, dtype=F32)
    inv_freq = ROPE_BASE ** (-jnp.arange(half, dtype=F32) / half)
    ang = pos[:, None] * inv_freq[None, :]
    cos = jnp.cos(ang)
    sin = jnp.sin(ang)
    cos2 = jnp.concatenate([cos, cos], axis=-1)
    sin2 = jnp.concatenate([-sin, sin], axis=-1)
    c = RET_CHUNK
    h = RET_HEADS
    log_gamma = jnp.log1p(-jnp.exp2(-5.0 - jnp.arange(h, dtype=F32)))
    idx = jnp.arange(c, dtype=F32)
    rel = idx[:, None] - idx[None, :]
    din = jnp.where(rel >= 0, jnp.exp(log_gamma[:, None, None] * jnp.maximum(rel, 0.0)), 0.0)
    dq = jnp.exp(log_gamma[:, None] * (idx + 1.0))
    dk = jnp.exp(log_gamma[:, None] * (c - 1.0 - idx))
    dc = jnp.exp(log_gamma * c)
    dq_b = jnp.broadcast_to(dq[:, :, None], (h, c, RET_V_DIM))
    dk_b = jnp.broadcast_to(dk[:, :, None], (h, c, RET_QK_DIM))
    dc_b = jnp.broadcast_to(dc[:, None, None], (h, RET_QK_DIM, RET_V_DIM))
    return cos2, sin2, din, dq_b, dk_b, dc_b


def _retention(u3, gn_w):
    b, s, _ = u3.shape
    ts = TS_RET
    qk_w = RET_HEADS * RET_QK_DIM
    v_w = RET_HEADS * RET_V_DIM
    cos2, sin2, din, dq_b, dk_b, dc_b = _retention_tables(s)
    q_blk, k_blk = 0, 1
    v_blk = (2 * qk_w) // v_w
    g_blk = v_blk + 1
    seq_spec = lambda w, blk: pl.BlockSpec((1, ts, w), lambda i, j: (i, j, blk))
    return pl.pallas_call(
        functools.partial(_retention_kernel, chunk=RET_CHUNK),
        grid=(b, s // ts),
        in_specs=[
            seq_spec(qk_w, q_blk), seq_spec(qk_w, k_blk), seq_spec(v_w, v_blk), seq_spec(v_w, g_blk),
            pl.BlockSpec((ts, RET_QK_DIM), lambda i, j: (j, 0)),
            pl.BlockSpec((ts, RET_QK_DIM), lambda i, j: (j, 0)),
            _const_spec(din.shape), _const_spec(dq_b.shape), _const_spec(dk_b.shape),
            _const_spec(dc_b.shape), _const_spec((1, v_w)),
        ],
        out_specs=pl.BlockSpec((1, ts, v_w), lambda i, j: (i, j, 0)),
        out_shape=jax.ShapeDtypeStruct((b, s, v_w), BF16),
        scratch_shapes=[pltpu.VMEM((RET_HEADS, RET_QK_DIM, RET_V_DIM), F32)],
        compiler_params=_params(32, 2),
        name="retention",
    )(u3, u3, u3, u3, cos2, sin2, din, dq_b, dk_b, dc_b, gn_w)


def _conv_kernel(glu_ref, w_ref, bias_ref, lnw_ref, lnb_ref, o_ref, ext_ref, conv_ref):
    ts = glu_ref.shape[1]
    ch = glu_ref.shape[2]
    n_slabs = ch // LANES
    first = HALO - (CONV_WIDTH - 1)

    @pl.when(pl.program_id(1) == 0)
    def _():
        ext_ref[:, 0:HALO, :] = jnp.zeros((n_slabs, HALO, LANES), F32)

    @pl.when(pl.program_id(1) > 0)
    def _():
        ext_ref[:, 0:HALO, :] = ext_ref[:, ts:ts + HALO, :]

    for l in range(n_slabs):
        ext_ref[l, HALO:HALO + ts, :] = glu_ref[0, :, l * LANES:(l + 1) * LANES].astype(F32)

    def slab_body(l, carry):
        taps = [w_ref[l, j:j + 1, :] for j in range(CONV_WIDTH)]
        bias = jnp.broadcast_to(bias_ref[l], (CONV_ROWS, LANES))
        for r0 in range(0, ts, CONV_ROWS):
            acc = bias
            for j in range(CONV_WIDTH):
                acc = acc + taps[j] * ext_ref[l, r0 + first + j:r0 + first + j + CONV_ROWS, :]
            conv_ref[l, r0:r0 + CONV_ROWS, :] = acc
        return carry

    lax.fori_loop(0, n_slabs, slab_body, 0)

    c = jnp.concatenate([conv_ref[l] for l in range(n_slabs)], axis=1)
    mu = jnp.mean(c, axis=-1, keepdims=True)
    cc = c - mu
    var = jnp.mean(cc * cc, axis=-1, keepdims=True)
    y = cc * lax.rsqrt(var + EPS) * lnw_ref[...] + lnb_ref[...]
    o_ref[0] = (y * _sigmoid(y)).astype(BF16)


def _conv_branch(u3, w_dw, b_dw, ln_w, ln_b):
    b, s, _ = u3.shape
    ch = w_dw.shape[1]
    n_slabs = ch // LANES
    ts = TS_CONV
    w_slabs = w_dw.reshape(CONV_WIDTH, n_slabs, LANES).transpose(1, 0, 2)
    b_slabs = b_dw.reshape(n_slabs, 1, LANES)
    return pl.pallas_call(
        _conv_kernel,
        grid=(b, s // ts),
        in_specs=[
            pl.BlockSpec((1, ts, ch), lambda i, j: (i, j, U_GLU)),
            _const_spec(w_slabs.shape), _const_spec(b_slabs.shape), _const_spec((1, ch)), _const_spec((1, ch)),
        ],
        out_specs=pl.BlockSpec((1, ts, ch), lambda i, j: (i, j, 0)),
        out_shape=jax.ShapeDtypeStruct((b, s, ch), BF16),
        scratch_shapes=[pltpu.VMEM((n_slabs, ts + HALO, LANES), F32), pltpu.VMEM((n_slabs, ts, LANES), F32)],
        compiler_params=_params(32, 2),
        name="conv_branch",
    )(u3, w_slabs, b_slabs, ln_w, ln_b)


def _mix_route_kernel(gret_ref, cact_ref, mret_ref, mconv_ref, x_ref, wro_ref, wco_ref, bco_ref,
                      wo_ref, gffn_ref, wrh_ref, wrl_ref, br_ref, upper_ref, lower_ref,
                      x1_ref, h2_ref, rrow_ref, rcol_ref, cnt_ref):
    tm = x_ref.shape[0]
    y_ret = jnp.dot(gret_ref[...], wro_ref[...], preferred_element_type=F32)
    y_conv = jnp.dot(cact_ref[...], wco_ref[...], preferred_element_type=F32) + bco_ref[...]
    mixed = (_sigmoid(mret_ref[...].astype(F32)) * y_ret
             + _sigmoid(mconv_ref[...].astype(F32)) * y_conv)
    x1 = x_ref[...] + jnp.dot(mixed.astype(BF16), wo_ref[...], preferred_element_type=F32)
    x1_ref[...] = x1
    h2 = _rms(x1, gffn_ref[...])
    hh = h2.astype(BF16)
    h2_ref[...] = hh

    hl = (h2 - hh.astype(F32)).astype(BF16)
    nt = (((1,), (1,)), ((), ()))
    logits = (lax.dot_general(wrh_ref[...], hh, nt, preferred_element_type=F32)
              + lax.dot_general(wrl_ref[...], hh, nt, preferred_element_type=F32)
              + lax.dot_general(wrh_ref[...], hl, nt, preferred_element_type=F32)) + br_ref[...]
    erow = lax.broadcasted_iota(I32, (N_EXPERTS, tm), 0)

    vals = logits
    idxs, tops = [], []
    for _ in range(TOP_K):
        m = jnp.max(vals, axis=0, keepdims=True)
        i = jnp.min(jnp.where(vals == m, erow, N_EXPERTS), axis=0, keepdims=True)
        idxs.append(i)
        tops.append(m)
        vals = jnp.where(erow == i, -jnp.inf, vals)
    exps = [jnp.exp(m - tops[0]) for m in tops]
    denom = exps[0] + exps[1] + exps[2] + exps[3]
    wts = [e / denom for e in exps]

    hits = [erow == i for i in idxs]
    onehot = jnp.zeros((N_EXPERTS, tm), F32)
    for hit in hits:
        onehot = jnp.where(hit, 1.0, onehot)
    before = jnp.dot(onehot.astype(BF16), upper_ref[...], preferred_element_type=F32)
    n_e = jnp.sum(onehot, axis=1, keepdims=True).astype(I32)
    n_pad = (n_e + (RUN_ALIGN - 1)) & -RUN_ALIGN
    off = jnp.dot(lower_ref[...], jnp.broadcast_to(n_pad.astype(F32), (N_EXPERTS, LANES)).astype(BF16),
                  preferred_element_type=F32)[:, 0:1]
    slot = before + off
    poss = [jnp.sum(jnp.where(hit, slot, 0.0), axis=0, keepdims=True) for hit in hits]
    cnt_ref[0] = n_e

    r8 = lax.broadcasted_iota(I32, (SUBLANES, tm), 0)
    rrow = jnp.zeros((SUBLANES, tm), I32)
    rl = lax.broadcasted_iota(I32, (LANES, tm), 0)
    rcol_t = jnp.zeros((LANES, tm), F32)
    for k in range(TOP_K):
        rrow = jnp.where(r8 == k, poss[k].astype(I32), rrow)
        rcol_t = jnp.where(rl == k, poss[k], rcol_t)
        rcol_t = jnp.where(rl == TOP_K + k, wts[k], rcol_t)
    rrow_ref[...] = rrow
    rcol_ref[...] = rcol_t.T


def _mix_route(gret, cact, u, x2, wro, wco, bco, wo, gffn, wr_hi, wr_lo, br, mret_blk, mconv_blk):
    t, d = x2.shape
    tm = TM_MOE
    upper = jnp.asarray(np.triu(np.ones((tm, tm), np.float32), 1), BF16)
    lower = jnp.asarray(np.tril(np.ones((N_EXPERTS, N_EXPERTS), np.float32), -1), BF16)
    row = lambda w: pl.BlockSpec((tm, w), lambda i: (i, 0))
    col = lambda blk: pl.BlockSpec((tm, d), lambda i: (i, blk))
    return pl.pallas_call(
        _mix_route_kernel,
        grid=(t // tm,),
        in_specs=[
            row(d), row(d), col(mret_blk), col(mconv_blk), row(d),
            _const_spec((d, d)), _const_spec((d, d)), _const_spec((1, d)), _const_spec((d, d)),
            _const_spec((1, d)), _const_spec((N_EXPERTS, d)), _const_spec((N_EXPERTS, d)),
            _const_spec((N_EXPERTS, 1)), _const_spec((tm, tm)), _const_spec((N_EXPERTS, N_EXPERTS)),
        ],
        out_specs=[
            row(d), row(d),
            pl.BlockSpec((SUBLANES, tm), lambda i: (0, i)),
            row(LANES),
            pl.BlockSpec((1, N_EXPERTS, 1), lambda i: (i, 0, 0)),
        ],
        out_shape=[
            jax.ShapeDtypeStruct((t, d), F32), jax.ShapeDtypeStruct((t, d), BF16),
            jax.ShapeDtypeStruct((SUBLANES, t), I32), jax.ShapeDtypeStruct((t, LANES), F32),
            jax.ShapeDtypeStruct((t // tm, N_EXPERTS, 1), I32),
        ],
        compiler_params=_params(48),
        name="mix_route",
    )(gret, cact, u, u, x2, wro, wco, bco, wo, gffn, wr_hi, wr_lo, br, upper, lower)


def _moe_capacity(n_assign, n_tiles):
    worst = n_assign + n_tiles * N_EXPERTS * (RUN_ALIGN - 1) + N_EXPERTS * (MOE_BLOCK - 1)
    return -(-worst // MOE_BLOCK) * MOE_BLOCK


def _route_plan(cnt, n_assign):
    blk, a = MOE_BLOCK, RUN_ALIGN
    cnt_te = cnt[:, :, 0]
    rows_te = ((cnt_te + (a - 1)) // a) * a
    used = jnp.sum(rows_te, axis=0)
    padded = ((used + blk - 1) // blk) * blk
    pend = jnp.cumsum(padded)
    pstart = pend - padded
    gstart = pstart[None, :] + jnp.cumsum(rows_te, axis=0) - rows_te
    off = jnp.cumsum(rows_te, axis=1) - rows_te
    cap = _moe_capacity(n_assign, cnt_te.shape[0])
    starts = jnp.arange(cap // blk, dtype=I32) * blk
    block_e = jnp.minimum(jnp.sum((pend[None, :] <= starts[:, None]).astype(I32), axis=1), N_EXPERTS - 1)
    n_used = pend[-1:] // blk
    tail = jnp.stack([pend[-1], (cap - pend[-1]) // blk])
    flat = lambda v: v.reshape(-1).astype(I32)
    tables = dict(rows=flat(rows_te), off=flat(off), g=flat(gstart), zs=flat(pstart + used),
                  zr=flat(padded - used), tail=flat(tail))
    return tables, block_e.astype(I32), n_used.astype(I32), cap


def _for_each_run(n_runs, rows_of, fn):
    def body(r, carry):
        rows = pl.multiple_of(rows_of(r), RUN_ALIGN)

        @pl.when(rows > 0)
        def _():
            fn(r, rows)

        return carry

    lax.fori_loop(0, n_runs, body, 0)


def _dispatch_kernel(rows_ref, off_ref, g_ref, zs_ref, zr_ref, tail_ref, rrow_ref, h2_ref, xb_ref,
                     sorted_ref, perm_ref, zero_ref, sem, zsem):
    i = pl.program_id(0)
    last = pl.num_programs(0) - 1
    tm = h2_ref.shape[0]
    slot = lax.rem(i, 2)

    def zero_copy(e, rows):
        dst = pl.multiple_of(zs_ref[e], RUN_ALIGN)
        return pltpu.make_async_copy(zero_ref.at[pl.ds(0, rows)], xb_ref.at[pl.ds(dst, rows)], zsem)

    def tail_copy(j):
        dst = pl.multiple_of(tail_ref[0] + j * MOE_BLOCK, MOE_BLOCK)
        return pltpu.make_async_copy(zero_ref, xb_ref.at[pl.ds(dst, MOE_BLOCK)], zsem)

    @pl.when(i == 0)
    def _():
        zero_ref[...] = jnp.zeros_like(zero_ref)
        pad_rows = lambda e: zr_ref[e]
        _for_each_run(N_EXPERTS, pad_rows, lambda e, rows: zero_copy(e, rows).start())
        lax.fori_loop(0, tail_ref[1], lambda j, c: (tail_copy(j).start(), c)[1], 0)
        _for_each_run(N_EXPERTS, pad_rows, lambda e, rows: zero_copy(e, rows).wait())
        lax.fori_loop(0, tail_ref[1], lambda j, c: (tail_copy(j).wait(), c)[1], 0)

    rrow = rrow_ref[...]
    chunk_rows = lax.broadcasted_iota(I32, (PERM_ROWS, tm), 0).astype(F32).astype(BF16)
    for c in range(SORT_ROWS // PERM_ROWS):
        perm = jnp.zeros((PERM_ROWS, tm), BF16)
        for k in range(TOP_K):
            rel = (rrow[k:k + 1, :] - c * PERM_ROWS).astype(F32).astype(BF16)
            perm = jnp.where(chunk_rows == rel, jnp.ones((), BF16), perm)
        perm_ref[c * PERM_ROWS:(c + 1) * PERM_ROWS, :] = perm
    sorted_ref[slot] = jnp.dot(perm_ref[...], h2_ref[...], preferred_element_type=F32)

    def run_copy(tile, e, rows):
        r = tile * N_EXPERTS + e
        src = pl.multiple_of(off_ref[r], RUN_ALIGN)
        dst = pl.multiple_of(g_ref[r], RUN_ALIGN)
        return pltpu.make_async_copy(sorted_ref.at[lax.rem(tile, 2), pl.ds(src, rows)],
                                     xb_ref.at[pl.ds(dst, rows)], sem)

    def for_runs_of(tile, fn):
        _for_each_run(N_EXPERTS, lambda e: rows_ref[tile * N_EXPERTS + e], lambda e, rows: fn(run_copy(tile, e, rows)))

    @pl.when(i > 0)
    def _():
        for_runs_of(i - 1, lambda cp: cp.wait())

    for_runs_of(i, lambda cp: cp.start())

    @pl.when(i == last)
    def _():
        for_runs_of(i, lambda cp: cp.wait())


def _dispatch(tables, rrow, h2, cap):
    t, d = h2.shape
    tm = TM_MOE
    grid_spec = pltpu.PrefetchScalarGridSpec(
        num_scalar_prefetch=6,
        grid=(t // tm,),
        in_specs=[
            pl.BlockSpec((SUBLANES, tm), lambda i, *_: (0, i)),
            pl.BlockSpec((tm, d), lambda i, *_: (i, 0)),
        ],
        out_specs=pl.BlockSpec(memory_space=pl.ANY),
        scratch_shapes=[
            pltpu.VMEM((2, SORT_ROWS, d), F32),
            pltpu.VMEM((SORT_ROWS, tm), BF16),
            pltpu.VMEM((MOE_BLOCK, d), F32),
            pltpu.SemaphoreType.DMA, pltpu.SemaphoreType.DMA,
        ],
    )
    return pl.pallas_call(
        _dispatch_kernel,
        grid_spec=grid_spec,
        out_shape=jax.ShapeDtypeStruct((cap, d), F32),
        compiler_params=_params(40),
        name="dispatch",
    )(tables["rows"], tables["off"], tables["g"], tables["zs"], tables["zr"], tables["tail"], rrow, h2)


def _expert_kernel(be_ref, nu_ref, xb_ref, wgu_ref, bgu_ref, wd_ref, bd_ref, yb_ref):
    del be_ref
    b = pl.program_id(0)
    dff = wd_ref.shape[1]
    live = b < nu_ref[0]

    @pl.when(live)
    def _():
        gu = jnp.dot(xb_ref[...].astype(BF16), wgu_ref[0], preferred_element_type=F32) + bgu_ref[0]
        gate = jnp.minimum(gu[:, :dff], SWIGLU_LIMIT)
        up = jnp.clip(gu[:, dff:], -SWIGLU_LIMIT, SWIGLU_LIMIT)
        act = (up + 1.0) * gate * _sigmoid(SWIGLU_ALPHA * gate)
        yb_ref[...] = jnp.dot(act.astype(BF16), wd_ref[0], preferred_element_type=F32) + bd_ref[0]

    @pl.when(jnp.logical_not(live))
    def _():
        yb_ref[...] = jnp.zeros_like(yb_ref)


def _experts(block_e, n_used, xb, wgu, bgu, wd, bd):
    cap, d = xb.shape
    dff2 = wgu.shape[2]
    blk = MOE_BLOCK
    grid_spec = pltpu.PrefetchScalarGridSpec(
        num_scalar_prefetch=2,
        grid=(cap // blk,),
        in_specs=[
            pl.BlockSpec((blk, d), lambda b, be, nu: (b, 0)),
            pl.BlockSpec((1, d, dff2), lambda b, be, nu: (be[b], 0, 0)),
            pl.BlockSpec((1, 1, dff2), lambda b, be, nu: (be[b], 0, 0)),
            pl.BlockSpec((1, dff2 // 2, d), lambda b, be, nu: (be[b], 0, 0)),
            pl.BlockSpec((1, 1, d), lambda b, be, nu: (be[b], 0, 0)),
        ],
        out_specs=pl.BlockSpec((blk, d), lambda b, be, nu: (b, 0)),
    )
    return pl.pallas_call(
        _expert_kernel,
        grid_spec=grid_spec,
        out_shape=jax.ShapeDtypeStruct((cap, d), F32),
        compiler_params=_params(56),
        name="experts",
    )(block_e, n_used, xb, wgu, bgu, wd, bd)


def _combine_kernel(rows_ref, off_ref, g_ref, rcol_ref, x1_ref, p_ref, gple_ref, wpg_ref, wpp_ref,
                    gfin_ref, yb_ref, out_ref, ys_ref, sems):
    i = pl.program_id(0)
    n_steps = pl.num_programs(0)
    tm = x1_ref.shape[0]
    slot = lax.rem(i, 2)

    def run_copy(tile, e, rows):
        r = tile * N_EXPERTS + e
        src = pl.multiple_of(g_ref[r], RUN_ALIGN)
        dst = pl.multiple_of(off_ref[r], RUN_ALIGN)
        s = lax.rem(tile, 2)
        return pltpu.make_async_copy(yb_ref.at[pl.ds(src, rows)], ys_ref.at[s, pl.ds(dst, rows)], sems.at[s])

    def for_runs_of(tile, fn):
        _for_each_run(N_EXPERTS, lambda e: rows_ref[tile * N_EXPERTS + e], lambda e, rows: fn(run_copy(tile, e, rows)))

    @pl.when(i == 0)
    def _():
        ys_ref[...] = jnp.zeros_like(ys_ref)
        for_runs_of(0, lambda cp: cp.start())

    @pl.when(i + 1 < n_steps)
    def _():
        for_runs_of(i + 1, lambda cp: cp.start())

    for_runs_of(i, lambda cp: cp.wait())

    rcol = rcol_ref[...]
    chunk_cols = lax.broadcasted_iota(I32, (tm, PERM_ROWS), 1).astype(F32).astype(BF16)
    wts = [jnp.broadcast_to(rcol[:, TOP_K + k:TOP_K + k + 1], (tm, PERM_ROWS)).astype(BF16) for k in range(TOP_K)]
    y = jnp.zeros(x1_ref.shape, F32)
    for c in range(SORT_ROWS // PERM_ROWS):
        perm = jnp.zeros((tm, PERM_ROWS), BF16)
        for k in range(TOP_K):
            rel = jnp.broadcast_to(rcol[:, k:k + 1] - c * PERM_ROWS, (tm, PERM_ROWS)).astype(BF16)
            perm = jnp.where(chunk_cols == rel, wts[k], perm)
        rows = ys_ref[slot, c * PERM_ROWS:(c + 1) * PERM_ROWS, :].astype(BF16)
        y = y + jnp.dot(perm, rows, preferred_element_type=F32)
    x2 = x1_ref[...] + y
    h3 = _rms(x2, gple_ref[...]).astype(BF16)
    gate = _sigmoid(jnp.dot(h3, wpg_ref[...], preferred_element_type=F32))
    proj = jnp.dot(p_ref[...].astype(BF16), wpp_ref[...], preferred_element_type=F32)
    x3 = x2 + gate * proj
    out_ref[...] = _rms(x3, gfin_ref[...])


def _combine(tables, rcol, x1, p2, gple, wpg, wpp, gfin, yb):
    t, d = x1.shape
    pd = p2.shape[1]
    tm = TM_MOE
    row = lambda w: pl.BlockSpec((tm, w), lambda i, *_: (i, 0))
    const = lambda shape: pl.BlockSpec(shape, lambda i, *_: (0,) * len(shape))
    grid_spec = pltpu.PrefetchScalarGridSpec(
        num_scalar_prefetch=3,
        grid=(t // tm,),
        in_specs=[
            row(LANES), row(d), row(pd),
            const((1, d)), const((d, d)), const((pd, d)), const((1, d)),
            pl.BlockSpec(memory_space=pl.ANY),
        ],
        out_specs=row(d),
        scratch_shapes=[
            pltpu.VMEM((2, SORT_ROWS, d), F32),
            pltpu.SemaphoreType.DMA((2,)),
        ],
    )
    return pl.pallas_call(
        _combine_kernel,
        grid_spec=grid_spec,
        out_shape=jax.ShapeDtypeStruct((t, d), F32),
        compiler_params=_params(56),
        name="combine",
    )(tables["rows"], tables["off"], tables["g"], rcol, x1, p2, gple, wpg, wpp, gfin, yb)


def _layer(x, p_i, g_mix, w_in, ret_gn_w, w_ret_out, w_dw, b_dw, conv_ln_w, conv_ln_b, w_conv_out,
           b_conv_out, w_o, g_ffn, w_router, b_router, w_gu, b_gu, w_down, b_down, g_ple,
           w_ple_gate, w_ple_proj, g_final):
    b, s, d = x.shape
    t = b * s
    x2 = x.reshape(t, d)
    row = lambda v: v.reshape(1, -1)

    u, w_gu_bf, w_down_bf = _in_proj(x2, row(g_mix), w_in.astype(BF16), w_gu, w_down)
    u3 = u.reshape(b, s, -1)
    gret = _retention(u3, row(ret_gn_w)).reshape(t, -1)
    cact = _conv_branch(u3, w_dw, row(b_dw), row(conv_ln_w), row(conv_ln_b)).reshape(t, -1)

    wr_t = w_router.T
    wr_hi = wr_t.astype(BF16)
    wr_lo = (wr_t - wr_hi.astype(F32)).astype(BF16)
    x1, h2, rrow, rcol, cnt = _mix_route(
        gret, cact, u, x2, w_ret_out.astype(BF16), w_conv_out.astype(BF16), row(b_conv_out),
        w_o.astype(BF16), row(g_ffn), wr_hi, wr_lo, b_router.reshape(-1, 1), U_MERGE_RET, U_MERGE_CONV)

    tables, block_e, n_used, cap = _route_plan(cnt, t * TOP_K)
    xb = _dispatch(tables, rrow, h2, cap)
    yb = _experts(block_e, n_used, xb, w_gu_bf, b_gu[:, None, :], w_down_bf, b_down[:, None, :])
    out = _combine(tables, rcol, x1, p_i.reshape(t, -1), row(g_ple), w_ple_gate.astype(BF16),
                   w_ple_proj.astype(BF16), row(g_final), yb)
    return out.reshape(b, s, d)


def kernel(x, p, g_mix, w_in, ret_gn_w, w_ret_out, w_dw, b_dw, conv_ln_w, conv_ln_b, w_conv_out,
           b_conv_out, w_o, g_ffn, w_router, b_router, w_gu, b_gu, w_down, b_down, g_ple,
           w_ple_gate, w_ple_proj, g_final):
    assert p.shape[0] == 1, "single-layer block"
    return _layer(x, p[0], g_mix[0], w_in[0], ret_gn_w[0], w_ret_out[0], w_dw[0], b_dw[0],
                  conv_ln_w[0], conv_ln_b[0], w_conv_out[0], b_conv_out[0], w_o[0], g_ffn[0],
                  w_router[0], b_router[0], w_gu[0], b_gu[0], w_down[0], b_down[0], g_ple[0],
                  w_ple_gate[0], w_ple_proj[0], g_final)
```
